```python
import jax
import jax.numpy as jnp
from jax import lax
import numpy as np

D_MODEL = 1024
BATCH = 2
SEQ = 16384
DEPTH = 1
DEC_BATCH = 8
DEC_SEQ = 32
PAST_LEN = 1024

CHUNK = 64
N_HEADS_A = 8
HEAD_DIM_A = 64
WIDTH_A = N_HEADS_A * HEAD_DIM_A
N_GROUPS_B = 4
GROUP_DIM_B = 128
WIDTH_B = N_GROUPS_B * GROUP_DIM_B
GMLP_CHUNK = 128
D_FF = 2816
Q_BLOCK = 128
N_MOD = 9
EPS = 1e-6
ATTN_SCALE = HEAD_DIM_A ** -0.5
IN_COLS = 3 * WIDTH_A + N_HEADS_A + 2 * WIDTH_B + 2 * D_MODEL
IN_SPLITS = (WIDTH_A, 2 * WIDTH_A, 3 * WIDTH_A, 3 * WIDTH_A + N_HEADS_A,
             3 * WIDTH_A + N_HEADS_A + 2 * WIDTH_B,
             3 * WIDTH_A + N_HEADS_A + 2 * WIDTH_B + D_MODEL)

kernel_name = 'fox_gmlp_macaron_adaln_stream_step'


def rms_norm(x, g):
    xf = x.astype(jnp.float32)
    y = xf * lax.rsqrt(jnp.mean(xf * xf, axis=-1, keepdims=True) + EPS)
    return (y * g.astype(jnp.float32)).astype(x.dtype)


def modulate(x, shift, scale):
    return x * (1 + scale) + shift


def ada_modulation(c, w_ada, b_ada):
    m = jax.nn.silu(c) @ w_ada + b_ada
    return jnp.split(m[:, None, :], N_MOD, axis=-1)


def ffn_sublayer(x, shift, scale, gate, g, w_gate, w_up, w_down):
    h = modulate(rms_norm(x, g), shift, scale)
    return x + 0.5 * gate * ((jax.nn.silu(h @ w_gate) * (h @ w_up)) @ w_down)


def mixer_projections(n, w_in, b_forget, g_q, g_k, g_gmlp_v):
    B, T, _ = n.shape
    z = n @ w_in
    q, k, v, f, zb, ga, gb = jnp.split(z, IN_SPLITS, axis=-1)
    q = rms_norm(q.reshape(B, T, N_HEADS_A, HEAD_DIM_A), g_q)
    k = rms_norm(k.reshape(B, T, N_HEADS_A, HEAD_DIM_A), g_k)
    v = v.reshape(B, T, N_HEADS_A, HEAD_DIM_A)
    logf = jax.nn.log_sigmoid((f + b_forget).astype(jnp.float32))
    u, vb = jnp.split(jax.nn.gelu(zb), 2, axis=-1)
    vb = rms_norm(vb, g_gmlp_v)
    return q, k, v, logf, u, vb, ga, gb


def fox_prompt(q, k, v, logf):
    B, S, H, Dh = q.shape
    nb = S // Q_BLOCK
    cT = jnp.cumsum(logf, axis=1).transpose(0, 2, 1)
    qb = q.reshape(B, nb, Q_BLOCK, H, Dh).transpose(1, 0, 2, 3, 4)
    cqb = cT.reshape(B, H, nb, Q_BLOCK).transpose(2, 0, 1, 3)
    key_pos = jnp.arange(S)

    def block(args):
        i, qi, ci = args
        s = jnp.einsum('bqhd,bkhd->bhqk', qi, k, preferred_element_type=jnp.float32) * ATTN_SCALE
        s = s + ci[..., :, None] - cT[..., None, :]
        q_pos = i * Q_BLOCK + jnp.arange(Q_BLOCK)
        s = jnp.where(key_pos[None, :] <= q_pos[:, None], s, -jnp.inf)
        p = jax.nn.softmax(s, axis=-1)
        return jnp.einsum('bhqk,bkhd->bqhd', p.astype(v.dtype), v)

    out = lax.map(block, (jnp.arange(nb), qb, cqb))
    return out.transpose(1, 0, 2, 3, 4).reshape(B, S, H * Dh)


def fox_sample(q, k_new, v_new, logf_new, k_cache, v_cache, logf_cache):
    B, T, H, Dh = q.shape
    P = k_cache.shape[1]
    k_all = jnp.concatenate([k_cache.astype(k_new.dtype), k_new], axis=1)
    v_all = jnp.concatenate([v_cache.astype(v_new.dtype), v_new], axis=1)
    lf_all = jnp.concatenate([logf_cache.astype(jnp.float32), logf_new], axis=1)
    cT = jnp.cumsum(lf_all, axis=1).transpose(0, 2, 1)
    s = jnp.einsum('bqhd,bkhd->bhqk', q, k_all, preferred_element_type=jnp.float32) * ATTN_SCALE
    s = s + cT[..., P:, None] - cT[..., None, :]
    key_pos = jnp.arange(P + T)
    q_pos = P + jnp.arange(T)
    s = jnp.where(key_pos[None, :] <= q_pos[:, None], s, -jnp.inf)
    p = jax.nn.softmax(s, axis=-1)
    return jnp.einsum('bhqk,bkhd->bqhd', p.astype(v_all.dtype), v_all).reshape(B, T, H * Dh)


def causal_spatial(w_spatial):
    mask = jnp.tril(jnp.ones((GMLP_CHUNK, GMLP_CHUNK), dtype=bool))
    return jnp.where(mask[None], w_spatial, 0)


def gmlp_prompt(u, vb, w_spatial, b_spatial):
    B, S, _ = vb.shape
    nc = S // GMLP_CHUNK
    vr = vb.reshape(B, nc, GMLP_CHUNK, N_GROUPS_B, GROUP_DIM_B)
    mixed = jnp.einsum('gts,bcsgd->bctgd', causal_spatial(w_spatial), vr)
    mixed = mixed + b_spatial.T[None, None, :, :, None]
    return u * mixed.reshape(B, S, WIDTH_B)


def gmlp_sample(u, vb, w_spatial, b_spatial):
    B, T, _ = vb.shape
    ws = causal_spatial(w_spatial)[:, :T, :T]
    vr = vb.reshape(B, T, N_GROUPS_B, GROUP_DIM_B)
    mixed = jnp.einsum('gts,bsgd->btgd', ws, vr) + b_spatial[:, :T].T[None, :, :, None]
    return u * mixed.reshape(B, T, WIDTH_B)


def merge_branches(a, b, ga, gb, w_proj_a, w_proj_b, w_out):
    m = jax.nn.sigmoid(ga) * (a @ w_proj_a) + jax.nn.sigmoid(gb) * (b @ w_proj_b)
    return m @ w_out


def setup_inputs(seed: int = 0) -> dict:
    key = jax.random.key(seed)
    ks = jax.random.split(key, 28)

    def nrm(k, shape, s):
        return jax.random.normal(k, shape, jnp.float32) * s

    def gain(k, shape):
        return 1.0 + nrm(k, shape, 0.05)

    return {
        'x_prompt': nrm(ks[0], (BATCH, SEQ, D_MODEL), 1.0),
        'x_sample': nrm(ks[1], (DEC_BATCH, DEC_SEQ, D_MODEL), 1.0),
        'c_prompt': nrm(ks[2], (BATCH, D_MODEL), 1.0),
        'c_sample': nrm(ks[3], (DEC_BATCH, D_MODEL), 1.0),
        'cache_fox_k': nrm(ks[4], (DEPTH, DEC_BATCH, PAST_LEN, N_HEADS_A, HEAD_DIM_A), 1.0),
        'cache_fox_v': nrm(ks[5], (DEPTH, DEC_BATCH, PAST_LEN, N_HEADS_A, HEAD_DIM_A), 1.0),
        'cache_fox_logf': jax.nn.log_sigmoid(2.0 + nrm(ks[6], (DEPTH, DEC_BATCH, PAST_LEN, N_HEADS_A), 0.5)),
        'w_ada': nrm(ks[7], (DEPTH, D_MODEL, N_MOD * D_MODEL), 0.5 * D_MODEL ** -0.5),
        'b_ada': nrm(ks[8], (DEPTH, N_MOD * D_MODEL), 0.02),
        'g_norm_ffn1': gain(ks[9], (DEPTH, D_MODEL)),
        'w_ffn1_gate': nrm(ks[10], (DEPTH, D_MODEL, D_FF), D_MODEL ** -0.5),
        'w_ffn1_up': nrm(ks[11], (DEPTH, D_MODEL, D_FF), D_MODEL ** -0.5),
        'w_ffn1_down': nrm(ks[12], (DEPTH, D_FF, D_MODEL), D_FF ** -0.5),
        'g_norm_mix': gain(ks[13], (DEPTH, D_MODEL)),
        'w_in': nrm(ks[14], (DEPTH, D_MODEL, IN_COLS), D_MODEL ** -0.5),
        'b_forget': 2.0 + nrm(ks[15], (DEPTH, N_HEADS_A), 0.5),
        'g_q': gain(ks[16], (DEPTH, HEAD_DIM_A)),
        'g_k': gain(ks[17], (DEPTH, HEAD_DIM_A)),
        'g_gmlp_v': gain(ks[18], (DEPTH, WIDTH_B)),
        'w_spatial': nrm(ks[19], (DEPTH, N_GROUPS_B, GMLP_CHUNK, GMLP_CHUNK), GMLP_CHUNK ** -0.5),
        'b_spatial': 1.0 + nrm(ks[20], (DEPTH, N_GROUPS_B, GMLP_CHUNK), 0.1),
        'w_proj_a': nrm(ks[21], (DEPTH, WIDTH_A, D_MODEL), WIDTH_A ** -0.5),
        'w_proj_b': nrm(ks[22], (DEPTH, WIDTH_B, D_MODEL), WIDTH_B ** -0.5),
        'w_out': nrm(ks[23], (DEPTH, D_MODEL, D_MODEL), D_MODEL ** -0.5),
        'g_norm_ffn2': gain(ks[24], (DEPTH, D_MODEL)),
        'w_ffn2_gate': nrm(ks[25], (DEPTH, D_MODEL, D_FF), D_MODEL ** -0.5),
        'w_ffn2_up': nrm(ks[26], (DEPTH, D_MODEL, D_FF), D_MODEL ** -0.5),
        'w_ffn2_down': nrm(ks[27], (DEPTH, D_FF, D_MODEL), D_FF ** -0.5),
    }


def reference(x_prompt, x_sample, c_prompt, c_sample, cache_fox_k, cache_fox_v, cache_fox_logf,
              w_ada, b_ada, g_norm_ffn1, w_ffn1_gate, w_ffn1_up, w_ffn1_down,
              g_norm_mix, w_in, b_forget, g_q, g_k, g_gmlp_v, w_spatial, b_spatial,
              w_proj_a, w_proj_b, w_out, g_norm_ffn2, w_ffn2_gate, w_ffn2_up, w_ffn2_down):
    xp, xs = x_prompt, x_sample
    kp_l, vp_l, fp_l, ks_l, vs_l, fs_l, gs_l = [], [], [], [], [], [], []
    for l in range(DEPTH):
        mp = ada_modulation(c_prompt, w_ada[l], b_ada[l])
        ms = ada_modulation(c_sample, w_ada[l], b_ada[l])
        ffn1 = (g_norm_ffn1[l], w_ffn1_gate[l], w_ffn1_up[l], w_ffn1_down[l])
        ffn2 = (g_norm_ffn2[l], w_ffn2_gate[l], w_ffn2_up[l], w_ffn2_down[l])
        proj = (w_in[l], b_forget[l], g_q[l], g_k[l], g_gmlp_v[l])
        outp = (w_proj_a[l], w_proj_b[l], w_out[l])

        xp = ffn_sublayer(xp, mp[0], mp[1], mp[2], *ffn1)
        xs = ffn_sublayer(xs, ms[0], ms[1], ms[2], *ffn1)

        n = modulate(rms_norm(xp, g_norm_mix[l]), mp[3], mp[4])
        q, k, v, lf, u, vb, ga, gb = mixer_projections(n, *proj)
        a = fox_prompt(q, k, v, lf)
        b = gmlp_prompt(u, vb, w_spatial[l], b_spatial[l])
        xp = xp + mp[5] * merge_branches(a, b, ga, gb, *outp)
        kp_l.append(k)
        vp_l.append(v)
        fp_l.append(lf)

        n = modulate(rms_norm(xs, g_norm_mix[l]), ms[3], ms[4])
        q, k, v, lf, u, vb, ga, gb = mixer_projections(n, *proj)
        a = fox_sample(q, k, v, lf, cache_fox_k[l], cache_fox_v[l], cache_fox_logf[l])
        b = gmlp_sample(u, vb, w_spatial[l], b_spatial[l])
        xs = xs + ms[5] * merge_branches(a, b, ga, gb, *outp)
        ks_l.append(k)
        vs_l.append(v)
        fs_l.append(lf)
        gs_l.append(vb)

        xp = ffn_sublayer(xp, mp[6], mp[7], mp[8], *ffn2)
        xs = ffn_sublayer(xs, ms[6], ms[7], ms[8], *ffn2)

    return (xp, xs, jnp.stack(kp_l), jnp.stack(vp_l), jnp.stack(fp_l),
            jnp.stack(ks_l), jnp.stack(vs_l), jnp.stack(fs_l), jnp.stack(gs_l))
```

```python
import functools

import jax
import jax.numpy as jnp
from jax import lax
from jax.experimental import pallas as pl
from jax.experimental.pallas import tpu as pltpu

D_MODEL = 1024
N_HEADS = 8
HEAD_DIM = 64
WIDTH_A = N_HEADS * HEAD_DIM
N_GROUPS = 4
GROUP_DIM = 128
WIDTH_B = N_GROUPS * GROUP_DIM
GMLP_CHUNK = 128
D_FF = 2816
N_MOD = 9
EPS = 1e-6
ATTN_SCALE = HEAD_DIM ** -0.5

LANES = 128
N_PAIRS = N_HEADS // 2
VMEM_LIMIT = 56 * 1024 * 1024

F32 = jnp.float32
BF16 = jnp.bfloat16


def _const_spec(shape):
    nd = len(shape)
    return pl.BlockSpec(shape, lambda *_: (0,) * nd, pipeline_mode=pl.Buffered(1))


def _params(sem):
    return pltpu.CompilerParams(dimension_semantics=sem, vmem_limit_bytes=VMEM_LIMIT)


def _rms(x):
    return x * lax.rsqrt(jnp.mean(x * x, axis=-1, keepdims=True) + EPS)


def _split3(c):
    hi = c.astype(BF16).astype(F32)
    r = c - hi
    mid = r.astype(BF16).astype(F32)
    lo = (r - mid).astype(BF16).astype(F32)
    return hi, mid, lo


def _tril_ones(n, period=None):
    row = lax.broadcasted_iota(jnp.int32, (n, n), 0)
    col = lax.broadcasted_iota(jnp.int32, (n, n), 1)
    keep = col <= row
    if period is not None and period != n:
        keep = keep & ((row // period) == (col // period))
    return keep


def _cumsum_rows(x):
    n = x.shape[0]
    t = jnp.where(_tril_ones(n), 1.0, 0.0).astype(BF16)
    out = None
    for piece in _split3(x):
        d = jnp.dot(t, piece.astype(BF16), preferred_element_type=F32)
        out = d if out is None else out + d
    return out


def _head_norm(slab, gain):
    lane = lax.broadcasted_iota(jnp.int32, (1, LANES), 1)
    lo = lane < HEAD_DIM
    sq = slab * slab
    ss_lo = jnp.sum(jnp.where(lo, sq, 0.0), axis=-1, keepdims=True)
    ss_hi = jnp.sum(jnp.where(lo, 0.0, sq), axis=-1, keepdims=True)
    r = jnp.where(lo, lax.rsqrt(ss_lo / HEAD_DIM + EPS), lax.rsqrt(ss_hi / HEAD_DIM + EPS))
    return slab * r * gain


def _head_tiles(slab, fill_even, fill_odd):
    lane = lax.broadcasted_iota(jnp.int32, (1, LANES), 1)
    lo = lane < HEAD_DIM
    even = jnp.where(lo, slab, fill_even)
    odd = jnp.where(lo, pltpu.roll(slab, HEAD_DIM, 1), fill_odd)
    return even, odd


def _aug_q(pieces, h):
    lane = lax.broadcasted_iota(jnp.int32, (1, LANES), 1)
    hi, mid, lo = (p[:, h:h + 1] for p in pieces)
    ones = jnp.where((lane >= HEAD_DIM + 3) & (lane < HEAD_DIM + 6), 1.0, 0.0)
    return jnp.where(lane == HEAD_DIM, hi, jnp.where(lane == HEAD_DIM + 1, mid,
                     jnp.where(lane == HEAD_DIM + 2, lo, ones)))


def _aug_k(pieces, h):
    lane = lax.broadcasted_iota(jnp.int32, (1, LANES), 1)
    hi, mid, lo = (-p[:, h:h + 1] for p in pieces)
    ones = jnp.where((lane >= HEAD_DIM) & (lane < HEAD_DIM + 3), 1.0, 0.0)
    return jnp.where(lane == HEAD_DIM + 3, hi, jnp.where(lane == HEAD_DIM + 4, mid,
                     jnp.where(lane == HEAD_DIM + 5, lo, ones)))


def _log_sigmoid(x):
    return jnp.minimum(x, 0.0) - jnp.log1p(jnp.exp(-jnp.abs(x)))


def _ada_kernel(c_ref, w_ref, b_ref, o_ref):
    h = jax.nn.silu(c_ref[...]).astype(BF16)
    o_ref[...] = jnp.dot(h, w_ref[...].astype(BF16), preferred_element_type=F32) + b_ref[...]


def _ada(c, w_ada, b_ada):
    rows = c.shape[0]
    n = w_ada.shape[1]
    tn = D_MODEL
    return pl.pallas_call(
        _ada_kernel,
        grid=(n // tn,),
        in_specs=[pl.BlockSpec((rows, D_MODEL), lambda j: (0, 0)),
                  pl.BlockSpec((D_MODEL, tn), lambda j: (0, j)),
                  pl.BlockSpec((1, tn), lambda j: (0, j))],
        out_specs=pl.BlockSpec((rows, tn), lambda j: (0, j)),
        out_shape=jax.ShapeDtypeStruct((rows, n), F32),
        compiler_params=_params(("arbitrary",)),
        name="ada_modulation",
    )(c, w_ada, b_ada.reshape(1, n))


def _ffn_body(x, shift, scale, gate, g, wgu_ref, wd_ref):
    h = (_rms(x) * g) * (1.0 + scale) + shift
    z = jnp.dot(h.astype(BF16), wgu_ref[...], preferred_element_type=F32)
    a = jax.nn.silu(z[:, :D_FF]) * z[:, D_FF:]
    y = jnp.dot(a.astype(BF16), wd_ref[...], preferred_element_type=F32)
    return x + 0.5 * gate * y


def _ffn_kernel(x_ref, shift_ref, scale_ref, gate_ref, g_ref, wgu_ref, wd_ref, o_ref):
    o_ref[0] = _ffn_body(x_ref[0], shift_ref[0], scale_ref[0], gate_ref[0], g_ref[...], wgu_ref, wd_ref)


def _mod_spec(mod, tm):
    if mod.shape[1] == 1:
        return pl.BlockSpec((1, 1, D_MODEL), lambda b, i: (b, 0, 0))
    return pl.BlockSpec((1, tm, D_MODEL), lambda b, i: (b, i, 0))


def _ffn(x, shift, scale, gate, g, wgu, wd, tm):
    nb, s, _ = x.shape
    row = pl.BlockSpec((1, tm, D_MODEL), lambda b, i: (b, i, 0))
    return pl.pallas_call(
        _ffn_kernel,
        grid=(nb, s // tm),
        in_specs=[row, _mod_spec(shift, tm), _mod_spec(scale, tm), _mod_spec(gate, tm),
                  _const_spec((1, D_MODEL)), _const_spec(wgu.shape), _const_spec(wd.shape)],
        out_specs=row,
        out_shape=jax.ShapeDtypeStruct(x.shape, F32),
        compiler_params=_params(("arbitrary", "arbitrary")),
        name="ffn_half_step",
    )(x, shift, scale, gate, g.reshape(1, D_MODEL), wgu, wd)


def _mix_kernel(x_ref, shift_ref, scale_ref, g_ref, wcat_ref, bf_ref, gq_ref, gk_ref, gv_ref,
                wsp_ref, bsp_ref, wpb_ref, *rest, tm, mix_len, mix_period, prompt):
    if prompt:
        qa_ref, ka_ref, va_ref, ko_ref, vo_ref, lf_ref, gas_ref, mb_ref, carry_ref = rest
    else:
        qo_ref, ko_ref, vo_ref, lf_ref, vbn_ref, gas_ref, mb_ref = rest

    x = x_ref[0]
    n = (_rms(x) * g_ref[...]) * (1.0 + scale_ref[0]) + shift_ref[0]
    z = jnp.dot(n.astype(BF16), wcat_ref[...], preferred_element_type=F32)
    o_q, o_k, o_v = 0, WIDTH_A, 2 * WIDTH_A
    o_zb = 3 * WIDTH_A
    o_ga = o_zb + 2 * WIDTH_B
    o_gb = o_ga + D_MODEL
    o_f = o_gb + D_MODEL

    lane = lax.broadcasted_iota(jnp.int32, (1, LANES), 1)
    logf = jnp.where(lane < N_HEADS, _log_sigmoid(z[:, o_f:o_f + LANES] + bf_ref[...]), 0.0)
    lf_ref[0] = logf

    if prompt:
        @pl.when(pl.program_id(1) == 0)
        def _():
            carry_ref[...] = jnp.zeros_like(carry_ref)
        c = _cumsum_rows(logf) + carry_ref[...]
        carry_ref[...] = c[tm - 1:tm, :]
        pieces = _split3(c)

    for p in range(N_PAIRS):
        sl = slice(p * LANES, (p + 1) * LANES)
        qn = _head_norm(z[:, o_q + p * LANES:o_q + (p + 1) * LANES], gq_ref[...])
        kn = _head_norm(z[:, o_k + p * LANES:o_k + (p + 1) * LANES], gk_ref[...])
        v = z[:, o_v + p * LANES:o_v + (p + 1) * LANES]
        ko_ref[0, :, sl] = kn
        vo_ref[0, :, sl] = v
        if prompt:
            he, ho = 2 * p, 2 * p + 1
            qe, qo = _head_tiles(qn * ATTN_SCALE, _aug_q(pieces, he), _aug_q(pieces, ho))
            ke, ko = _head_tiles(kn, _aug_k(pieces, he), _aug_k(pieces, ho))
            ve, vo = _head_tiles(v, 1.0, 1.0)
            qa_ref[0, he] = qe.astype(BF16)
            qa_ref[0, ho] = qo.astype(BF16)
            ka_ref[0, he] = ke.astype(BF16)
            ka_ref[0, ho] = ko.astype(BF16)
            va_ref[0, he] = ve.astype(BF16)
            va_ref[0, ho] = vo.astype(BF16)
        else:
            qo_ref[0, :, sl] = qn

    zb = jax.nn.gelu(z[:, o_zb:o_zb + 2 * WIDTH_B])
    u = zb[:, :WIDTH_B]
    vbn = _rms(zb[:, WIDTH_B:]) * gv_ref[...]
    if not prompt:
        vbn_ref[0] = vbn
    vbb = vbn.astype(BF16)
    keep = _tril_ones(mix_len, mix_period)
    wm = [jnp.where(keep, wsp_ref[g], 0.0).astype(BF16) for g in range(N_GROUPS)]
    rows = []
    for c0 in range(0, tm, mix_len):
        cols = [jnp.dot(wm[g], vbb[c0:c0 + mix_len, g * GROUP_DIM:(g + 1) * GROUP_DIM],
                        preferred_element_type=F32) for g in range(N_GROUPS)]
        rows.append(jnp.concatenate(cols, axis=1) + bsp_ref[...])
    mixed = rows[0] if len(rows) == 1 else jnp.concatenate(rows, axis=0)
    b = (u * mixed).astype(BF16)
    pb = jnp.dot(b, wpb_ref[...], preferred_element_type=F32)
    mb_ref[0] = (jax.nn.sigmoid(z[:, o_gb:o_gb + D_MODEL]) * pb).astype(BF16)
    gas_ref[0] = jax.nn.sigmoid(z[:, o_ga:o_ga + D_MODEL]).astype(BF16)


def _mix(x, shift, scale, g, wcat, bf, gq, gk, gv, wsp, bsp, wpb, *, tm, mix_len, mix_period, prompt):
    nb, s, _ = x.shape
    row = lambda w: pl.BlockSpec((1, tm, w), lambda b, i: (b, i, 0))
    head = pl.BlockSpec((1, N_HEADS, tm, LANES), lambda b, i: (b, 0, i, 0))
    sds = jax.ShapeDtypeStruct
    tok = lambda w, dt: sds((nb, s, w), dt)
    hm = sds((nb, N_HEADS, s, LANES), BF16)
    if prompt:
        out_specs = [head, head, head, row(WIDTH_A), row(WIDTH_A), row(LANES), row(D_MODEL), row(D_MODEL)]
        out_shape = [hm, hm, hm, tok(WIDTH_A, F32), tok(WIDTH_A, F32), tok(LANES, F32),
                     tok(D_MODEL, BF16), tok(D_MODEL, BF16)]
        scratch = [pltpu.VMEM((1, LANES), F32)]
    else:
        out_specs = [row(WIDTH_A), row(WIDTH_A), row(WIDTH_A), row(LANES), row(WIDTH_B),
                     row(D_MODEL), row(D_MODEL)]
        out_shape = [tok(WIDTH_A, F32), tok(WIDTH_A, F32), tok(WIDTH_A, F32), tok(LANES, F32),
                     tok(WIDTH_B, F32), tok(D_MODEL, BF16), tok(D_MODEL, BF16)]
        scratch = []
    kern = functools.partial(_mix_kernel, tm=tm, mix_len=mix_len, mix_period=mix_period, prompt=prompt)
    return pl.pallas_call(
        kern,
        grid=(nb, s // tm),
        in_specs=[row(D_MODEL), _mod_spec(shift, tm), _mod_spec(scale, tm), _const_spec((1, D_MODEL)),
                  _const_spec(wcat.shape), _const_spec(bf.shape), _const_spec(gq.shape),
                  _const_spec(gk.shape), _const_spec(gv.shape), _const_spec(wsp.shape),
                  _const_spec(bsp.shape), _const_spec(wpb.shape)],
        out_specs=out_specs,
        out_shape=out_shape,
        scratch_shapes=scratch,
        compiler_params=_params(("arbitrary", "arbitrary")),
        name="mixer_in_prompt" if prompt else "mixer_in_sample",
    )(x, shift, scale, g.reshape(1, D_MODEL), wcat, bf, gq, gk, gv, wsp, bsp, wpb)


def _softmax_step(q, k, v, m_old, acc_old, mask):
    s = lax.dot_general(q, k, (((1,), (1,)), ((), ())), preferred_element_type=F32)
    if mask is not None:
        s = jnp.where(mask, s, -jnp.inf)
    m_new = jnp.maximum(m_old, jnp.max(s, axis=-1, keepdims=True))
    p = jnp.exp(s - m_new)
    alpha = jnp.exp(m_old - m_new)
    acc = alpha * acc_old + jnp.dot(p.astype(BF16), v, preferred_element_type=F32)
    return m_new, acc


def _pair_output(acc_even, acc_odd):
    lane = lax.broadcasted_iota(jnp.int32, (1, LANES), 1)
    oe = acc_even * pltpu.roll(1.0 / acc_even, HEAD_DIM, 1)
    oo = acc_odd * pltpu.roll(1.0 / acc_odd, HEAD_DIM, 1)
    return jnp.where(lane < HEAD_DIM, oe, pltpu.roll(oo, HEAD_DIM, 1))


def _attn_kernel(q_ref, k_ref, v_ref, o_ref, acc_ref, m_ref, *, t):
    qi = pl.program_id(2)
    m_ref[...] = jnp.full(m_ref.shape, -jnp.inf, F32)
    acc_ref[...] = jnp.zeros(acc_ref.shape, F32)

    def step(ki, mask):
        start = pl.multiple_of(ki * t, t)
        for h in range(2):
            k = k_ref[0, h, pl.ds(start, t), :]
            v = v_ref[0, h, pl.ds(start, t), :]
            m_new, acc = _softmax_step(q_ref[0, h], k, v, m_ref[h], acc_ref[h], mask)
            m_ref[h] = m_new
            acc_ref[h] = acc

    def body(ki, carry):
        step(ki, None)
        return carry

    lax.fori_loop(0, qi, body, 0)
    step(qi, _tril_ones(t))
    o_ref[0] = _pair_output(acc_ref[0], acc_ref[1]).astype(o_ref.dtype)


def _attn(qa, ka, va, t):
    nb, _, s, _ = qa.shape
    kv = pl.BlockSpec((1, 2, s, LANES), lambda b, p, i: (b, p, 0, 0))
    return pl.pallas_call(
        functools.partial(_attn_kernel, t=t),
        grid=(nb, N_PAIRS, s // t),
        in_specs=[pl.BlockSpec((1, 2, t, LANES), lambda b, p, i: (b, p, i, 0)), kv, kv],
        out_specs=pl.BlockSpec((1, t, LANES), lambda b, p, i: (b, i, p)),
        out_shape=jax.ShapeDtypeStruct((nb, s, WIDTH_A), BF16),
        scratch_shapes=[pltpu.VMEM((2, t, LANES), F32), pltpu.VMEM((2, t, 1), F32)],
        compiler_params=_params(("arbitrary", "arbitrary", "arbitrary")),
        name="fox_prompt_attention",
    )(qa, ka, va)


def _attn_sample_kernel(q_ref, k_ref, v_ref, lf_ref, kc_ref, vc_ref, lfc_ref, o_ref, *, t_new):
    c_cache = _cumsum_rows(lfc_ref[0])
    p_len = c_cache.shape[0]
    c_new = _cumsum_rows(lf_ref[0]) + c_cache[p_len - 1:p_len, :]
    pc_cache = _split3(c_cache)
    pc_new = _split3(c_new)
    causal = _tril_ones(t_new)
    for p in range(N_PAIRS):
        sl = slice(p * LANES, (p + 1) * LANES)
        he, ho = 2 * p, 2 * p + 1
        q_t = _head_tiles(q_ref[0, :, sl] * ATTN_SCALE, _aug_q(pc_new, he), _aug_q(pc_new, ho))
        kc_t = _head_tiles(kc_ref[0, :, sl], _aug_k(pc_cache, he), _aug_k(pc_cache, ho))
        kn_t = _head_tiles(k_ref[0, :, sl], _aug_k(pc_new, he), _aug_k(pc_new, ho))
        vc_t = _head_tiles(vc_ref[0, :, sl], 1.0, 1.0)
        vn_t = _head_tiles(v_ref[0, :, sl], 1.0, 1.0)
        accs = []
        for j in range(2):
            q = q_t[j].astype(BF16)
            m0 = jnp.full((t_new, 1), -jnp.inf, F32)
            a0 = jnp.zeros((t_new, LANES), F32)
            m1, a1 = _softmax_step(q, kc_t[j].astype(BF16), vc_t[j].astype(BF16), m0, a0, None)
            _, a2 = _softmax_step(q, kn_t[j].astype(BF16), vn_t[j].astype(BF16), m1, a1, causal)
            accs.append(a2)
        o_ref[0, :, sl] = _pair_output(accs[0], accs[1]).astype(o_ref.dtype)


def _attn_sample(q, k, v, lf, kc, vc, lfc):
    nb, t_new, _ = q.shape
    p_len = kc.shape[1]
    new = lambda w: pl.BlockSpec((1, t_new, w), lambda b: (b, 0, 0))
    old = lambda w: pl.BlockSpec((1, p_len, w), lambda b: (b, 0, 0))
    return pl.pallas_call(
        functools.partial(_attn_sample_kernel, t_new=t_new),
        grid=(nb,),
        in_specs=[new(WIDTH_A), new(WIDTH_A), new(WIDTH_A), new(LANES), old(WIDTH_A), old(WIDTH_A), old(LANES)],
        out_specs=new(WIDTH_A),
        out_shape=jax.ShapeDtypeStruct((nb, t_new, WIDTH_A), BF16),
        compiler_params=_params(("arbitrary",)),
        name="fox_sample_attention",
    )(q, k, v, lf, kc, vc, lfc)


def _merge_kernel(a_ref, gas_ref, mb_ref, x_ref, gate_ref, wpa_ref, wo_ref, o_ref):
    pa = jnp.dot(a_ref[0], wpa_ref[...], preferred_element_type=F32)
    m = gas_ref[0].astype(F32) * pa + mb_ref[0].astype(F32)
    y = jnp.dot(m.astype(BF16), wo_ref[...], preferred_element_type=F32)
    o_ref[0] = x_ref[0] + gate_ref[0] * y


def _merge(a, gas, mb, x, gate, wpa, wo, tm):
    nb, s, _ = x.shape
    row = lambda w: pl.BlockSpec((1, tm, w), lambda b, i: (b, i, 0))
    return pl.pallas_call(
        _merge_kernel,
        grid=(nb, s // tm),
        in_specs=[row(WIDTH_A), row(D_MODEL), row(D_MODEL), row(D_MODEL), _mod_spec(gate, tm),
                  _const_spec(wpa.shape), _const_spec(wo.shape)],
        out_specs=row(D_MODEL),
        out_shape=jax.ShapeDtypeStruct(x.shape, F32),
        compiler_params=_params(("arbitrary", "arbitrary")),
        name="branch_merge",
    )(a, gas, mb, x, gate, wpa, wo)


def _pad_lanes(v, fill=0.0):
    v = v.reshape(1, -1)
    return jnp.pad(v, ((0, 0), (0, LANES - v.shape[1])), constant_values=fill)


def _layer(xp, xs, mp, ms, cache_k, cache_v, cache_lf, g1, wg1, wu1, wd1, gm, w_in, b_forget, g_q, g_k, g_v,
           w_spatial, b_spatial, w_proj_a, w_proj_b, w_out, g2, wg2, wu2, wd2, *, tm, t_attn):
    nb_s, t_new, _ = xs.shape
    rows_s = nb_s * t_new
    xs = xs.reshape(1, rows_s, D_MODEL)
    ms = [jnp.repeat(m, t_new, axis=0).reshape(1, rows_s, D_MODEL) for m in ms]
    mp = [m[:, None, :] for m in mp]

    wgu1 = jnp.concatenate([wg1, wu1], axis=1).astype(BF16)
    wgu2 = jnp.concatenate([wg2, wu2], axis=1).astype(BF16)
    wd1 = wd1.astype(BF16)
    wd2 = wd2.astype(BF16)
    o_f = 3 * WIDTH_A
    o_zb = o_f + N_HEADS
    wf = jnp.pad(w_in[:, o_f:o_zb], ((0, 0), (0, LANES - N_HEADS)))
    wcat = jnp.concatenate([w_in[:, :o_f], w_in[:, o_zb:], wf], axis=1).astype(BF16)
    bf = _pad_lanes(b_forget)
    gq = jnp.tile(g_q, 2).reshape(1, LANES)
    gk = jnp.tile(g_k, 2).reshape(1, LANES)
    gv = g_v.reshape(1, WIDTH_B)
    bsp = jnp.repeat(b_spatial.T, GROUP_DIM, axis=1)
    wpa = w_proj_a.astype(BF16)
    wpb = w_proj_b.astype(BF16)
    wo = w_out.astype(BF16)

    xp = _ffn(xp, mp[0], mp[1], mp[2], g1, wgu1, wd1, tm)
    xs = _ffn(xs, ms[0], ms[1], ms[2], g1, wgu1, wd1, rows_s)

    qa, ka, va, k_p, v_p, lf_p, gas, mb = _mix(
        xp, mp[3], mp[4], gm, wcat, bf, gq, gk, gv, w_spatial, bsp, wpb,
        tm=tm, mix_len=GMLP_CHUNK, mix_period=GMLP_CHUNK, prompt=True)
    a = _attn(qa, ka, va, t_attn)
    xp = _merge(a, gas, mb, xp, mp[5], wpa, wo, tm)

    wsp_s = jnp.tile(w_spatial[:, :t_new, :t_new], (1, nb_s, nb_s))
    bsp_s = jnp.tile(bsp[:t_new], (nb_s, 1))
    q_s, k_s, v_s, lf_s, vbn_s, gas_s, mb_s = _mix(
        xs, ms[3], ms[4], gm, wcat, bf, gq, gk, gv, wsp_s, bsp_s, wpb,
        tm=rows_s, mix_len=rows_s, mix_period=t_new, prompt=False)
    p_len = cache_k.shape[1]
    per_req = lambda z: z.reshape(nb_s, t_new, z.shape[-1])
    lfc = jnp.pad(cache_lf, ((0, 0), (0, 0), (0, LANES - N_HEADS)))
    a_s = _attn_sample(per_req(q_s), per_req(k_s), per_req(v_s), per_req(lf_s),
                       cache_k.reshape(nb_s, p_len, WIDTH_A), cache_v.reshape(nb_s, p_len, WIDTH_A), lfc)
    xs = _merge(a_s.reshape(1, rows_s, WIDTH_A), gas_s, mb_s, xs, ms[5], wpa, wo, rows_s)

    xp = _ffn(xp, mp[6], mp[7], mp[8], g2, wgu2, wd2, tm)
    xs = _ffn(xs, ms[6], ms[7], ms[8], g2, wgu2, wd2, rows_s)

    nb_p, s_p, _ = xp.shape
    heads = lambda z, nb, s: z.reshape(nb, s, N_HEADS, HEAD_DIM)
    outs = (heads(k_p, nb_p, s_p), heads(v_p, nb_p, s_p), lf_p[:, :, :N_HEADS],
            heads(per_req(k_s), nb_s, t_new), heads(per_req(v_s), nb_s, t_new),
            per_req(lf_s)[:, :, :N_HEADS], per_req(vbn_s))
    return xp, xs.reshape(nb_s, t_new, D_MODEL), outs


def kernel(x_prompt, x_sample, c_prompt, c_sample, cache_fox_k, cache_fox_v, cache_fox_logf, w_ada, b_ada, g_norm_ffn1, w_ffn1_gate, w_ffn1_up, w_ffn1_down, g_norm_mix, w_in, b_forget, g_q, g_k, g_gmlp_v, w_spatial, b_spatial, w_proj_a, w_proj_b, w_out, g_norm_ffn2, w_ffn2_gate, w_ffn2_up, w_ffn2_down):
    depth = w_ada.shape[0]
    nb_p, s_p, _ = x_prompt.shape
    nb_s = x_sample.shape[0]
    tm = min(512, s_p)
    t_attn = min(512, s_p)
    xp, xs = x_prompt, x_sample
    c_all = jnp.concatenate([c_prompt, c_sample], axis=0)
    rows = c_all.shape[0]
    c_all = jnp.pad(c_all, ((0, -rows % 8), (0, 0)))
    stacks = [[] for _ in range(7)]
    for l in range(depth):
        mod = _ada(c_all, w_ada[l], b_ada[l])
        mp = [mod[:nb_p, i * D_MODEL:(i + 1) * D_MODEL] for i in range(N_MOD)]
        ms = [mod[nb_p:nb_p + nb_s, i * D_MODEL:(i + 1) * D_MODEL] for i in range(N_MOD)]
        xp, xs, outs = _layer(
            xp, xs, mp, ms, cache_fox_k[l], cache_fox_v[l], cache_fox_logf[l],
            g_norm_ffn1[l], w_ffn1_gate[l], w_ffn1_up[l], w_ffn1_down[l],
            g_norm_mix[l], w_in[l], b_forget[l], g_q[l], g_k[l], g_gmlp_v[l], w_spatial[l], b_spatial[l],
            w_proj_a[l], w_proj_b[l], w_out[l],
            g_norm_ffn2[l], w_ffn2_gate[l], w_ffn2_up[l], w_ffn2_down[l], tm=tm, t_attn=t_attn)
        for st, o in zip(stacks, outs):
            st.append(o)
    return (xp, xs) + tuple(jnp.stack(st) for st in stacks)
```

```python
import functools

import jax
import jax.numpy as jnp
from jax import lax
from jax.experimental import pallas as pl
from jax.experimental.pallas import tpu as pltpu

D_MODEL = 1024
N_HEADS = 8
HEAD_DIM = 64
WIDTH_A = N_HEADS * HEAD_DIM
N_GROUPS = 4
GROUP_DIM = 128
WIDTH_B = N_GROUPS * GROUP_DIM
GMLP_CHUNK = 128
D_FF = 2816
N_MOD = 9
EPS = 1e-6
ATTN_SCALE = HEAD_DIM ** -0.5
LOG2E = 1.4426950408889634

LANES = 128
N_PAIRS = N_HEADS // 2
VMEM_LIMIT = 56 * 1024 * 1024

F32 = jnp.float32
BF16 = jnp.bfloat16


def _const_spec(shape):
    nd = len(shape)
    return pl.BlockSpec(shape, lambda *_: (0,) * nd, pipeline_mode=pl.Buffered(1))


def _params(sem):
    return pltpu.CompilerParams(dimension_semantics=sem, vmem_limit_bytes=VMEM_LIMIT)


def _rms(x):
    return x * lax.rsqrt(jnp.mean(x * x, axis=-1, keepdims=True) + EPS)


def _split3(c):
    hi = c.astype(BF16).astype(F32)
    r = c - hi
    mid = r.astype(BF16).astype(F32)
    lo = (r - mid).astype(BF16).astype(F32)
    return hi, mid, lo


def _tril_ones(n, period=None):
    row = lax.broadcasted_iota(jnp.int32, (n, n), 0)
    col = lax.broadcasted_iota(jnp.int32, (n, n), 1)
    keep = col <= row
    if period is not None and period != n:
        keep = keep & ((row // period) == (col // period))
    return keep


def _cumsum_rows(x):
    n = x.shape[0]
    t = jnp.where(_tril_ones(n), 1.0, 0.0).astype(BF16)
    out = None
    for piece in _split3(x):
        d = jnp.dot(t, piece.astype(BF16), preferred_element_type=F32)
        out = d if out is None else out + d
    return out


def _head_norm(slab, gain):
    lane = lax.broadcasted_iota(jnp.int32, (1, LANES), 1)
    lo = lane < HEAD_DIM
    sq = slab * slab
    ss_lo = jnp.sum(jnp.where(lo, sq, 0.0), axis=-1, keepdims=True)
    ss_hi = jnp.sum(jnp.where(lo, 0.0, sq), axis=-1, keepdims=True)
    r = jnp.where(lo, lax.rsqrt(ss_lo / HEAD_DIM + EPS), lax.rsqrt(ss_hi / HEAD_DIM + EPS))
    return slab * r * gain


def _head_tiles(slab, fill_even, fill_odd):
    lane = lax.broadcasted_iota(jnp.int32, (1, LANES), 1)
    lo = lane < HEAD_DIM
    even = jnp.where(lo, slab, fill_even)
    odd = jnp.where(lo, pltpu.roll(slab, HEAD_DIM, 1), fill_odd)
    return even, odd


def _aug_q(pieces, h):
    lane = lax.broadcasted_iota(jnp.int32, (1, LANES), 1)
    hi, mid, lo = (p[:, h:h + 1] for p in pieces)
    ones = jnp.where((lane >= HEAD_DIM + 3) & (lane < HEAD_DIM + 6), 1.0, 0.0)
    return jnp.where(lane == HEAD_DIM, hi, jnp.where(lane == HEAD_DIM + 1, mid,
                     jnp.where(lane == HEAD_DIM + 2, lo, ones)))


def _aug_k(pieces, h):
    lane = lax.broadcasted_iota(jnp.int32, (1, LANES), 1)
    hi, mid, lo = (-p[:, h:h + 1] for p in pieces)
    ones = jnp.where((lane >= HEAD_DIM) & (lane < HEAD_DIM + 3), 1.0, 0.0)
    return jnp.where(lane == HEAD_DIM + 3, hi, jnp.where(lane == HEAD_DIM + 4, mid,
                     jnp.where(lane == HEAD_DIM + 5, lo, ones)))


def _log_sigmoid(x):
    return jnp.minimum(x, 0.0) - jnp.log1p(jnp.exp(-jnp.abs(x)))


def _ada_kernel(c_ref, w_ref, b_ref, o_ref):
    h = jax.nn.silu(c_ref[...]).astype(BF16)
    o_ref[...] = jnp.dot(h, w_ref[...].astype(BF16), preferred_element_type=F32) + b_ref[...]


def _ada(c, w_ada, b_ada):
    rows = c.shape[0]
    n = w_ada.shape[1]
    tn = D_MODEL
    return pl.pallas_call(
        _ada_kernel,
        grid=(n // tn,),
        in_specs=[pl.BlockSpec((rows, D_MODEL), lambda j: (0, 0)),
                  pl.BlockSpec((D_MODEL, tn), lambda j: (0, j)),
                  pl.BlockSpec((1, tn), lambda j: (0, j))],
        out_specs=pl.BlockSpec((rows, tn), lambda j: (0, j)),
        out_shape=jax.ShapeDtypeStruct((rows, n), F32),
        compiler_params=_params(("arbitrary",)),
        name="ada_modulation",
    )(c, w_ada, b_ada.reshape(1, n))


def _ffn_body(x, shift, scale, gate, g, wgu_ref, wd_ref):
    h = (_rms(x) * g) * (1.0 + scale) + shift
    z = jnp.dot(h.astype(BF16), wgu_ref[...], preferred_element_type=F32)
    a = jax.nn.silu(z[:, :D_FF]) * z[:, D_FF:]
    y = jnp.dot(a.astype(BF16), wd_ref[...], preferred_element_type=F32)
    return x + 0.5 * gate * y


def _ffn_kernel(x_ref, shift_ref, scale_ref, gate_ref, g_ref, wgu_ref, wd_ref, o_ref):
    o_ref[0] = _ffn_body(x_ref[0], shift_ref[0], scale_ref[0], gate_ref[0], g_ref[...], wgu_ref, wd_ref)


def _mod_spec(mod, tm):
    if mod.shape[1] == 1:
        return pl.BlockSpec((1, 1, D_MODEL), lambda b, i: (b, 0, 0))
    return pl.BlockSpec((1, tm, D_MODEL), lambda b, i: (b, i, 0))


def _ffn(x, shift, scale, gate, g, wgu, wd, tm):
    nb, s, _ = x.shape
    row = pl.BlockSpec((1, tm, D_MODEL), lambda b, i: (b, i, 0))
    return pl.pallas_call(
        _ffn_kernel,
        grid=(nb, s // tm),
        in_specs=[row, _mod_spec(shift, tm), _mod_spec(scale, tm), _mod_spec(gate, tm),
                  _const_spec((1, D_MODEL)), _const_spec(wgu.shape), _const_spec(wd.shape)],
        out_specs=row,
        out_shape=jax.ShapeDtypeStruct(x.shape, F32),
        compiler_params=_params(("arbitrary", "arbitrary")),
        name="ffn_half_step",
    )(x, shift, scale, gate, g.reshape(1, D_MODEL), wgu, wd)


def _mix_kernel(x_ref, shift_ref, scale_ref, g_ref, wcat_ref, bf_ref, gq_ref, gk_ref, gv_ref,
                wsp_ref, bsp_ref, wpb_ref, *rest, tm, mix_len, mix_period, prompt):
    if prompt:
        qa_ref, ka_ref, va_ref, ko_ref, vo_ref, lf_ref, gas_ref, mb_ref, carry_ref = rest
    else:
        qo_ref, ko_ref, vo_ref, lf_ref, vbn_ref, gas_ref, mb_ref = rest

    x = x_ref[0]
    n = (_rms(x) * g_ref[...]) * (1.0 + scale_ref[0]) + shift_ref[0]
    z = jnp.dot(n.astype(BF16), wcat_ref[...], preferred_element_type=F32)
    o_q, o_k, o_v = 0, WIDTH_A, 2 * WIDTH_A
    o_zb = 3 * WIDTH_A
    o_ga = o_zb + 2 * WIDTH_B
    o_gb = o_ga + D_MODEL
    o_f = o_gb + D_MODEL

    lane = lax.broadcasted_iota(jnp.int32, (1, LANES), 1)
    logf = jnp.where(lane < N_HEADS, _log_sigmoid(z[:, o_f:o_f + LANES] + bf_ref[...]), 0.0)
    lf_ref[0] = logf

    if prompt:
        @pl.when(pl.program_id(1) == 0)
        def _():
            carry_ref[...] = jnp.zeros_like(carry_ref)
        c = _cumsum_rows(logf) + carry_ref[...]
        carry_ref[...] = c[tm - 1:tm, :]
        pieces = _split3(c * LOG2E)

    for p in range(N_PAIRS):
        sl = slice(p * LANES, (p + 1) * LANES)
        qn = _head_norm(z[:, o_q + p * LANES:o_q + (p + 1) * LANES], gq_ref[...])
        kn = _head_norm(z[:, o_k + p * LANES:o_k + (p + 1) * LANES], gk_ref[...])
        v = z[:, o_v + p * LANES:o_v + (p + 1) * LANES]
        ko_ref[0, :, sl] = kn
        vo_ref[0, :, sl] = v
        if prompt:
            he, ho = 2 * p, 2 * p + 1
            qe, qo = _head_tiles(qn * (ATTN_SCALE * LOG2E), _aug_q(pieces, he), _aug_q(pieces, ho))
            ke, ko = _head_tiles(kn, _aug_k(pieces, he), _aug_k(pieces, ho))
            ve, vo = _head_tiles(v, 1.0, 1.0)
            qa_ref[0, he] = qe.astype(BF16)
            qa_ref[0, ho] = qo.astype(BF16)
            ka_ref[0, he] = ke.astype(BF16)
            ka_ref[0, ho] = ko.astype(BF16)
            va_ref[0, he] = ve.astype(BF16)
            va_ref[0, ho] = vo.astype(BF16)
        else:
            qo_ref[0, :, sl] = qn

    zb = jax.nn.gelu(z[:, o_zb:o_zb + 2 * WIDTH_B])
    u = zb[:, :WIDTH_B]
    vbn = _rms(zb[:, WIDTH_B:]) * gv_ref[...]
    if not prompt:
        vbn_ref[0] = vbn
    vbb = vbn.astype(BF16)
    keep = _tril_ones(mix_len, mix_period)
    wm = [jnp.where(keep, wsp_ref[g], 0.0).astype(BF16) for g in range(N_GROUPS)]
    rows = []
    for c0 in range(0, tm, mix_len):
        cols = [jnp.dot(wm[g], vbb[c0:c0 + mix_len, g * GROUP_DIM:(g + 1) * GROUP_DIM],
                        preferred_element_type=F32) for g in range(N_GROUPS)]
        rows.append(jnp.concatenate(cols, axis=1) + bsp_ref[...])
    mixed = rows[0] if len(rows) == 1 else jnp.concatenate(rows, axis=0)
    b = (u * mixed).astype(BF16)
    pb = jnp.dot(b, wpb_ref[...], preferred_element_type=F32)
    mb_ref[0] = (jax.nn.sigmoid(z[:, o_gb:o_gb + D_MODEL]) * pb).astype(BF16)
    gas_ref[0] = jax.nn.sigmoid(z[:, o_ga:o_ga + D_MODEL]).astype(BF16)


def _mix(x, shift, scale, g, wcat, bf, gq, gk, gv, wsp, bsp, wpb, *, tm, mix_len, mix_period, prompt):
    nb, s, _ = x.shape
    row = lambda w: pl.BlockSpec((1, tm, w), lambda b, i: (b, i, 0))
    head = pl.BlockSpec((1, N_HEADS, tm, LANES), lambda b, i: (b, 0, i, 0))
    sds = jax.ShapeDtypeStruct
    tok = lambda w, dt: sds((nb, s, w), dt)
    hm = sds((nb, N_HEADS, s, LANES), BF16)
    if prompt:
        out_specs = [head, head, head, row(WIDTH_A), row(WIDTH_A), row(LANES), row(D_MODEL), row(D_MODEL)]
        out_shape = [hm, hm, hm, tok(WIDTH_A, F32), tok(WIDTH_A, F32), tok(LANES, F32),
                     tok(D_MODEL, BF16), tok(D_MODEL, BF16)]
        scratch = [pltpu.VMEM((1, LANES), F32)]
    else:
        out_specs = [row(WIDTH_A), row(WIDTH_A), row(WIDTH_A), row(LANES), row(WIDTH_B),
                     row(D_MODEL), row(D_MODEL)]
        out_shape = [tok(WIDTH_A, F32), tok(WIDTH_A, F32), tok(WIDTH_A, F32), tok(LANES, F32),
                     tok(WIDTH_B, F32), tok(D_MODEL, BF16), tok(D_MODEL, BF16)]
        scratch = []
    kern = functools.partial(_mix_kernel, tm=tm, mix_len=mix_len, mix_period=mix_period, prompt=prompt)
    return pl.pallas_call(
        kern,
        grid=(nb, s // tm),
        in_specs=[row(D_MODEL), _mod_spec(shift, tm), _mod_spec(scale, tm), _const_spec((1, D_MODEL)),
                  _const_spec(wcat.shape), _const_spec(bf.shape), _const_spec(gq.shape),
                  _const_spec(gk.shape), _const_spec(gv.shape), _const_spec(wsp.shape),
                  _const_spec(bsp.shape), _const_spec(wpb.shape)],
        out_specs=out_specs,
        out_shape=out_shape,
        scratch_shapes=scratch,
        compiler_params=_params(("arbitrary", "arbitrary")),
        name="mixer_in_prompt" if prompt else "mixer_in_sample",
    )(x, shift, scale, g.reshape(1, D_MODEL), wcat, bf, gq, gk, gv, wsp, bsp, wpb)


def _scores(q, k):
    return lax.dot_general(q, k, (((1,), (1,)), ((), ())), preferred_element_type=F32)


def _softmax_update(s, v, m_old, acc_old, mask):
    if mask is not None:
        s = jnp.where(mask, s, -jnp.inf)
    m_new = jnp.maximum(m_old, jnp.max(s, axis=-1, keepdims=True))
    p = jnp.exp2(s - m_new)
    alpha = jnp.exp2(m_old - m_new)
    acc = alpha * acc_old + jnp.dot(p.astype(BF16), v, preferred_element_type=F32)
    return m_new, acc


def _pair_output(acc_even, acc_odd):
    lane = lax.broadcasted_iota(jnp.int32, (1, LANES), 1)
    oe = acc_even * pltpu.roll(1.0 / acc_even, HEAD_DIM, 1)
    oo = acc_odd * pltpu.roll(1.0 / acc_odd, HEAD_DIM, 1)
    return jnp.where(lane < HEAD_DIM, oe, pltpu.roll(oo, HEAD_DIM, 1))


def _attn_kernel(q_ref, k_ref, v_ref, o_ref, s0_ref, s1_ref, acc_ref, m_ref, *, t):
    qi = pl.program_id(2)
    m_ref[...] = jnp.full(m_ref.shape, -jnp.inf, F32)
    acc_ref[...] = jnp.zeros(acc_ref.shape, F32)

    def rows(j):
        return pl.ds(pl.multiple_of(j * t, t), t)

    def qk(j, s_ref):
        for h in range(2):
            s_ref[h] = _scores(q_ref[0, h], k_ref[0, h, rows(j), :])

    def soft_pv(j, s_ref, mask):
        for h in range(2):
            m_new, acc = _softmax_update(s_ref[h], v_ref[0, h, rows(j), :], m_ref[h], acc_ref[h], mask)
            m_ref[h] = m_new
            acc_ref[h] = acc

    def body(u, carry):
        j = 2 * u
        qk(j + 1, s1_ref)
        soft_pv(j, s0_ref, None)
        qk(j + 2, s0_ref)
        soft_pv(j + 1, s1_ref, None)
        return carry

    causal = _tril_ones(t)
    qk(0, s0_ref)
    lax.fori_loop(0, qi // 2, body, 0)

    @pl.when(qi % 2 == 1)
    def _():
        qk(qi, s1_ref)
        soft_pv(qi - 1, s0_ref, None)
        soft_pv(qi, s1_ref, causal)

    @pl.when(qi % 2 == 0)
    def _():
        soft_pv(qi, s0_ref, causal)

    o_ref[0] = _pair_output(acc_ref[0], acc_ref[1]).astype(o_ref.dtype)


def _attn(qa, ka, va, t):
    nb, _, s, _ = qa.shape
    kv = pl.BlockSpec((1, 2, s, LANES), lambda b, p, i: (b, p, 0, 0))
    return pl.pallas_call(
        functools.partial(_attn_kernel, t=t),
        grid=(nb, N_PAIRS, s // t),
        in_specs=[pl.BlockSpec((1, 2, t, LANES), lambda b, p, i: (b, p, i, 0)), kv, kv],
        out_specs=pl.BlockSpec((1, t, LANES), lambda b, p, i: (b, i, p)),
        out_shape=jax.ShapeDtypeStruct((nb, s, WIDTH_A), BF16),
        scratch_shapes=[pltpu.VMEM((2, t, t), F32), pltpu.VMEM((2, t, t), F32),
                        pltpu.VMEM((2, t, LANES), F32), pltpu.VMEM((2, t, 1), F32)],
        compiler_params=_params(("arbitrary", "arbitrary", "arbitrary")),
        name="fox_prompt_attention",
    )(qa, ka, va)


def _attn_sample_kernel(q_ref, k_ref, v_ref, lf_ref, kc_ref, vc_ref, lfc_ref, o_ref, *, t_new):
    c_cache = _cumsum_rows(lfc_ref[0])
    p_len = c_cache.shape[0]
    c_new = _cumsum_rows(lf_ref[0]) + c_cache[p_len - 1:p_len, :]
    pc_cache = _split3(c_cache * LOG2E)
    pc_new = _split3(c_new * LOG2E)
    causal = _tril_ones(t_new)
    for p in range(N_PAIRS):
        sl = slice(p * LANES, (p + 1) * LANES)
        he, ho = 2 * p, 2 * p + 1
        q_t = _head_tiles(q_ref[0, :, sl] * (ATTN_SCALE * LOG2E), _aug_q(pc_new, he), _aug_q(pc_new, ho))
        kc_t = _head_tiles(kc_ref[0, :, sl], _aug_k(pc_cache, he), _aug_k(pc_cache, ho))
        kn_t = _head_tiles(k_ref[0, :, sl], _aug_k(pc_new, he), _aug_k(pc_new, ho))
        vc_t = _head_tiles(vc_ref[0, :, sl], 1.0, 1.0)
        vn_t = _head_tiles(v_ref[0, :, sl], 1.0, 1.0)
        accs = []
        for j in range(2):
            q = q_t[j].astype(BF16)
            m0 = jnp.full((t_new, 1), -jnp.inf, F32)
            a0 = jnp.zeros((t_new, LANES), F32)
            m1, a1 = _softmax_update(_scores(q, kc_t[j].astype(BF16)), vc_t[j].astype(BF16), m0, a0, None)
            _, a2 = _softmax_update(_scores(q, kn_t[j].astype(BF16)), vn_t[j].astype(BF16), m1, a1, causal)
            accs.append(a2)
        o_ref[0, :, sl] = _pair_output(accs[0], accs[1]).astype(o_ref.dtype)


def _attn_sample(q, k, v, lf, kc, vc, lfc):
    nb, t_new, _ = q.shape
    p_len = kc.shape[1]
    new = lambda w: pl.BlockSpec((1, t_new, w), lambda b: (b, 0, 0))
    old = lambda w: pl.BlockSpec((1, p_len, w), lambda b: (b, 0, 0))
    return pl.pallas_call(
        functools.partial(_attn_sample_kernel, t_new=t_new),
        grid=(nb,),
        in_specs=[new(WIDTH_A), new(WIDTH_A), new(WIDTH_A), new(LANES), old(WIDTH_A), old(WIDTH_A), old(LANES)],
        out_specs=new(WIDTH_A),
        out_shape=jax.ShapeDtypeStruct((nb, t_new, WIDTH_A), BF16),
        compiler_params=_params(("arbitrary",)),
        name="fox_sample_attention",
    )(q, k, v, lf, kc, vc, lfc)


def _merge_kernel(a_ref, gas_ref, mb_ref, x_ref, gate_ref, wpa_ref, wo_ref, o_ref):
    pa = jnp.dot(a_ref[0], wpa_ref[...], preferred_element_type=F32)
    m = gas_ref[0].astype(F32) * pa + mb_ref[0].astype(F32)
    y = jnp.dot(m.astype(BF16), wo_ref[...], preferred_element_type=F32)
    o_ref[0] = x_ref[0] + gate_ref[0] * y


def _merge(a, gas, mb, x, gate, wpa, wo, tm):
    nb, s, _ = x.shape
    row = lambda w: pl.BlockSpec((1, tm, w), lambda b, i: (b, i, 0))
    return pl.pallas_call(
        _merge_kernel,
        grid=(nb, s // tm),
        in_specs=[row(WIDTH_A), row(D_MODEL), row(D_MODEL), row(D_MODEL), _mod_spec(gate, tm),
                  _const_spec(wpa.shape), _const_spec(wo.shape)],
        out_specs=row(D_MODEL),
        out_shape=jax.ShapeDtypeStruct(x.shape, F32),
        compiler_params=_params(("arbitrary", "arbitrary")),
        name="branch_merge",
    )(a, gas, mb, x, gate, wpa, wo)


def _pad_lanes(v, fill=0.0):
    v = v.reshape(1, -1)
    return jnp.pad(v, ((0, 0), (0, LANES - v.shape[1])), constant_values=fill)


def _layer(xp, xs, mp, ms, cache_k, cache_v, cache_lf, g1, wg1, wu1, wd1, gm, w_in, b_forget, g_q, g_k, g_v,
           w_spatial, b_spatial, w_proj_a, w_proj_b, w_out, g2, wg2, wu2, wd2, *, tm, t_attn):
    nb_s, t_new, _ = xs.shape
    rows_s = nb_s * t_new
    xs = xs.reshape(1, rows_s, D_MODEL)
    ms = [jnp.repeat(m, t_new, axis=0).reshape(1, rows_s, D_MODEL) for m in ms]
    mp = [m[:, None, :] for m in mp]

    wgu1 = jnp.concatenate([wg1, wu1], axis=1).astype(BF16)
    wgu2 = jnp.concatenate([wg2, wu2], axis=1).astype(BF16)
    wd1 = wd1.astype(BF16)
    wd2 = wd2.astype(BF16)
    o_f = 3 * WIDTH_A
    o_zb = o_f + N_HEADS
    wf = jnp.pad(w_in[:, o_f:o_zb], ((0, 0), (0, LANES - N_HEADS)))
    wcat = jnp.concatenate([w_in[:, :o_f], w_in[:, o_zb:], wf], axis=1).astype(BF16)
    bf = _pad_lanes(b_forget)
    gq = jnp.tile(g_q, 2).reshape(1, LANES)
    gk = jnp.tile(g_k, 2).reshape(1, LANES)
    gv = g_v.reshape(1, WIDTH_B)
    bsp = jnp.repeat(b_spatial.T, GROUP_DIM, axis=1)
    wpa = w_proj_a.astype(BF16)
    wpb = w_proj_b.astype(BF16)
    wo = w_out.astype(BF16)

    xp = _ffn(xp, mp[0], mp[1], mp[2], g1, wgu1, wd1, tm)
    xs = _ffn(xs, ms[0], ms[1], ms[2], g1, wgu1, wd1, rows_s)

    qa, ka, va, k_p, v_p, lf_p, gas, mb = _mix(
        xp, mp[3], mp[4], gm, wcat, bf, gq, gk, gv, w_spatial, bsp, wpb,
        tm=tm, mix_len=GMLP_CHUNK, mix_period=GMLP_CHUNK, prompt=True)
    a = _attn(qa, ka, va, t_attn)
    xp = _merge(a, gas, mb, xp, mp[5], wpa, wo, tm)

    wsp_s = jnp.tile(w_spatial[:, :t_new, :t_new], (1, nb_s, nb_s))
    bsp_s = jnp.tile(bsp[:t_new], (nb_s, 1))
    q_s, k_s, v_s, lf_s, vbn_s, gas_s, mb_s = _mix(
        xs, ms[3], ms[4], gm, wcat, bf, gq, gk, gv, wsp_s, bsp_s, wpb,
        tm=rows_s, mix_len=rows_s, mix_period=t_new, prompt=False)
    p_len = cache_k.shape[1]
    per_req = lambda z: z.reshape(nb_s, t_new, z.shape[-1])
    lfc = jnp.pad(cache_lf, ((0, 0), (0, 0), (0, LANES - N_HEADS)))
    a_s = _attn_sample(per_req(q_s), per_req(k_s), per_req(v_s), per_req(lf_s),
                       cache_k.reshape(nb_s, p_len, WIDTH_A), cache_v.reshape(nb_s, p_len, WIDTH_A), lfc)
    xs = _merge(a_s.reshape(1, rows_s, WIDTH_A), gas_s, mb_s, xs, ms[5], wpa, wo, rows_s)

    xp = _ffn(xp, mp[6], mp[7], mp[8], g2, wgu2, wd2, tm)
    xs = _ffn(xs, ms[6], ms[7], ms[8], g2, wgu2, wd2, rows_s)

    nb_p, s_p, _ = xp.shape
    heads = lambda z, nb, s: z.reshape(nb, s, N_HEADS, HEAD_DIM)
    outs = (heads(k_p, nb_p, s_p), heads(v_p, nb_p, s_p), lf_p[:, :, :N_HEADS],
            heads(per_req(k_s), nb_s, t_new), heads(per_req(v_s), nb_s, t_new),
            per_req(lf_s)[:, :, :N_HEADS], per_req(vbn_s))
    return xp, xs.reshape(nb_s, t_new, D_MODEL), outs


def kernel(x_prompt, x_sample, c_prompt, c_sample, cache_fox_k, cache_fox_v, cache_fox_logf, w_ada, b_ada, g_norm_ffn1, w_ffn1_gate, w_ffn1_up, w_ffn1_down, g_norm_mix, w_in, b_forget, g_q, g_k, g_gmlp_v, w_spatial, b_spatial, w_proj_a, w_proj_b, w_out, g_norm_ffn2, w_ffn2_gate, w_ffn2_up, w_ffn2_down):
    depth = w_ada.shape[0]
    nb_p, s_p, _ = x_prompt.shape
    nb_s = x_sample.shape[0]
    tm = min(512, s_p)
    t_attn = min(512, s_p)
    xp, xs = x_prompt, x_sample
    c_all = jnp.concatenate([c_prompt, c_sample], axis=0)
    rows = c_all.shape[0]
    c_all = jnp.pad(c_all, ((0, -rows % 8), (0, 0)))
    stacks = [[] for _ in range(7)]
    for l in range(depth):
        mod = _ada(c_all, w_ada[l], b_ada[l])
        mp = [mod[:nb_p, i * D_MODEL:(i + 1) * D_MODEL] for i in range(N_MOD)]
        ms = [mod[nb_p:nb_p + nb_s, i * D_MODEL:(i + 1) * D_MODEL] for i in range(N_MOD)]
        xp, xs, outs = _layer(
            xp, xs, mp, ms, cache_fox_k[l], cache_fox_v[l], cache_fox_logf[l],
            g_norm_ffn1[l], w_ffn1_gate[l], w_ffn1_up[l], w_ffn1_down[l],
            g_norm_mix[l], w_in[l], b_forget[l], g_q[l], g_k[l], g_gmlp_v[l], w_spatial[l], b_spatial[l],
            w_proj_a[l], w_proj_b[l], w_out[l],
            g_norm_ffn2[l], w_ffn2_gate[l], w_ffn2_up[l], w_ffn2_down[l], tm=tm, t_attn=t_attn)
        for st, o in zip(stacks, outs):
            st.append(o)
    return (xp, xs) + tuple(jnp.stack(st) for st in stacks)
```

```python
import functools

import jax
import jax.numpy as jnp
from jax import lax
from jax.experimental import pallas as pl
from jax.experimental.pallas import tpu as pltpu

D_MODEL = 1024
N_HEADS = 8
HEAD_DIM = 64
WIDTH_A = N_HEADS * HEAD_DIM
N_GROUPS = 4
GROUP_DIM = 128
WIDTH_B = N_GROUPS * GROUP_DIM
GMLP_CHUNK = 128
D_FF = 2816
N_MOD = 9
EPS = 1e-6
ATTN_SCALE = HEAD_DIM ** -0.5
LOG2E = 1.4426950408889634

LANES = 128
SUBLANES = 8
N_PAIRS = N_HEADS // 2
VMEM_LIMIT = 56 * 1024 * 1024

F32 = jnp.float32
BF16 = jnp.bfloat16


def _const_spec(shape):
    nd = len(shape)
    return pl.BlockSpec(shape, lambda *_: (0,) * nd, pipeline_mode=pl.Buffered(1))


def _params(sem):
    return pltpu.CompilerParams(dimension_semantics=sem, vmem_limit_bytes=VMEM_LIMIT)


def _rms(x):
    return x * lax.rsqrt(jnp.mean(x * x, axis=-1, keepdims=True) + EPS)


def _split3(c):
    hi = c.astype(BF16).astype(F32)
    r = c - hi
    mid = r.astype(BF16).astype(F32)
    lo = (r - mid).astype(BF16).astype(F32)
    return hi, mid, lo


def _tril_ones(n, period=None):
    row = lax.broadcasted_iota(jnp.int32, (n, n), 0)
    col = lax.broadcasted_iota(jnp.int32, (n, n), 1)
    keep = col <= row
    if period is not None and period != n:
        keep = keep & ((row // period) == (col // period))
    return keep


def _cumsum_rows(x):
    n = x.shape[0]
    groups = n // SUBLANES
    x3 = x.reshape(groups, SUBLANES, LANES)
    sub = lax.broadcasted_iota(jnp.int32, (1, SUBLANES, LANES), 1)
    k = 1
    while k < SUBLANES:
        x3 = x3 + jnp.where(sub >= k, pltpu.roll(x3, k, 1), 0.0)
        k *= 2
    tot = x3[:, SUBLANES - 1:SUBLANES, :]
    k = 1
    while k < groups:
        tot = tot + jnp.concatenate([jnp.zeros((k, 1, LANES), F32), tot[:groups - k]], axis=0)
        k *= 2
    before = jnp.concatenate([jnp.zeros((1, 1, LANES), F32), tot[:groups - 1]], axis=0)
    return (x3 + before).reshape(n, LANES)


def _head_norm(slab, gain):
    lane = lax.broadcasted_iota(jnp.int32, (1, LANES), 1)
    lo = lane < HEAD_DIM
    sq = slab * slab
    ss_lo = jnp.sum(jnp.where(lo, sq, 0.0), axis=-1, keepdims=True)
    ss_hi = jnp.sum(jnp.where(lo, 0.0, sq), axis=-1, keepdims=True)
    r = jnp.where(lo, lax.rsqrt(ss_lo / HEAD_DIM + EPS), lax.rsqrt(ss_hi / HEAD_DIM + EPS))
    return slab * r * gain


def _head_tiles(slab, fill_even, fill_odd):
    lane = lax.broadcasted_iota(jnp.int32, (1, LANES), 1)
    lo = lane < HEAD_DIM
    even = jnp.where(lo, slab, fill_even)
    odd = jnp.where(lo, pltpu.roll(slab, HEAD_DIM, 1), fill_odd)
    return even, odd


def _aug_q(pieces, h):
    lane = lax.broadcasted_iota(jnp.int32, (1, LANES), 1)
    hi, mid, lo = (p[:, h:h + 1] for p in pieces)
    ones = jnp.where((lane >= HEAD_DIM + 3) & (lane < HEAD_DIM + 6), 1.0, 0.0)
    return jnp.where(lane == HEAD_DIM, hi, jnp.where(lane == HEAD_DIM + 1, mid,
                     jnp.where(lane == HEAD_DIM + 2, lo, ones)))


def _aug_k(pieces, h):
    lane = lax.broadcasted_iota(jnp.int32, (1, LANES), 1)
    hi, mid, lo = (-p[:, h:h + 1] for p in pieces)
    ones = jnp.where((lane >= HEAD_DIM) & (lane < HEAD_DIM + 3), 1.0, 0.0)
    return jnp.where(lane == HEAD_DIM + 3, hi, jnp.where(lane == HEAD_DIM + 4, mid,
                     jnp.where(lane == HEAD_DIM + 5, lo, ones)))


def _log_sigmoid(x):
    return jnp.minimum(x, 0.0) - jnp.log1p(jnp.exp(-jnp.abs(x)))


def _ada_kernel(c_ref, w_ref, b_ref, o_ref):
    h = jax.nn.silu(c_ref[...]).astype(BF16)
    o_ref[...] = jnp.dot(h, w_ref[...].astype(BF16), preferred_element_type=F32) + b_ref[...]


def _ada(c, w_ada, b_ada):
    rows = c.shape[0]
    n = w_ada.shape[1]
    tn = D_MODEL
    return pl.pallas_call(
        _ada_kernel,
        grid=(n // tn,),
        in_specs=[pl.BlockSpec((rows, D_MODEL), lambda j: (0, 0)),
                  pl.BlockSpec((D_MODEL, tn), lambda j: (0, j)),
                  pl.BlockSpec((1, tn), lambda j: (0, j))],
        out_specs=pl.BlockSpec((rows, tn), lambda j: (0, j)),
        out_shape=jax.ShapeDtypeStruct((rows, n), F32),
        compiler_params=_params(("arbitrary",)),
        name="ada_modulation",
    )(c, w_ada, b_ada.reshape(1, n))


def _ffn_body(x, shift, scale, gate, g, wgu_ref, wd_ref):
    h = (_rms(x) * g) * (1.0 + scale) + shift
    z = jnp.dot(h.astype(BF16), wgu_ref[...], preferred_element_type=F32)
    a = jax.nn.silu(z[:, :D_FF]) * z[:, D_FF:]
    y = jnp.dot(a.astype(BF16), wd_ref[...], preferred_element_type=F32)
    return x + 0.5 * gate * y


def _ffn_kernel(x_ref, shift_ref, scale_ref, gate_ref, g_ref, wgu_ref, wd_ref, o_ref):
    o_ref[0] = _ffn_body(x_ref[0], shift_ref[0], scale_ref[0], gate_ref[0], g_ref[...], wgu_ref, wd_ref)


def _mod_spec(mod, tm):
    if mod.shape[1] == 1:
        return pl.BlockSpec((1, 1, D_MODEL), lambda b, i: (b, 0, 0))
    return pl.BlockSpec((1, tm, D_MODEL), lambda b, i: (b, i, 0))


def _ffn(x, shift, scale, gate, g, wgu, wd, tm):
    nb, s, _ = x.shape
    row = pl.BlockSpec((1, tm, D_MODEL), lambda b, i: (b, i, 0))
    return pl.pallas_call(
        _ffn_kernel,
        grid=(nb, s // tm),
        in_specs=[row, _mod_spec(shift, tm), _mod_spec(scale, tm), _mod_spec(gate, tm),
                  _const_spec((1, D_MODEL)), _const_spec(wgu.shape), _const_spec(wd.shape)],
        out_specs=row,
        out_shape=jax.ShapeDtypeStruct(x.shape, F32),
        compiler_params=_params(("arbitrary", "arbitrary")),
        name="ffn_half_step",
    )(x, shift, scale, gate, g.reshape(1, D_MODEL), wgu, wd)


def _mix_project(x_ref, shift_ref, scale_ref, g_ref, wcat_ref, z_ref):
    n = (_rms(x_ref[0]) * g_ref[...]) * (1.0 + scale_ref[0]) + shift_ref[0]
    z_ref[...] = jnp.dot(n.astype(BF16), wcat_ref[...], preferred_element_type=F32)


def _mix_finish(z_ref, first_tile, bf_ref, gq_ref, gk_ref, gv_ref, wsp_ref, bsp_ref, wpb_ref, outs,
                *, tm, mix_len, mix_period, prompt):
    if prompt:
        qa_ref, ka_ref, va_ref, ko_ref, vo_ref, lf_ref, gas_ref, mb_ref, carry_ref = outs
    else:
        qo_ref, ko_ref, vo_ref, lf_ref, vbn_ref, gas_ref, mb_ref = outs
    o_q, o_k, o_v = 0, WIDTH_A, 2 * WIDTH_A
    o_zb = 3 * WIDTH_A
    o_ga = o_zb + 2 * WIDTH_B
    o_gb = o_ga + D_MODEL
    o_f = o_gb + D_MODEL

    lane = lax.broadcasted_iota(jnp.int32, (1, LANES), 1)
    logf = jnp.where(lane < N_HEADS, _log_sigmoid(z_ref[:, o_f:o_f + LANES] + bf_ref[...]), 0.0)
    lf_ref[0] = logf

    if prompt:
        c = _cumsum_rows(logf) + jnp.where(first_tile, 0.0, carry_ref[...])
        carry_ref[...] = c[tm - 1:tm, :]
        pieces = _split3(c * LOG2E)

    for p in range(N_PAIRS):
        sl = slice(p * LANES, (p + 1) * LANES)
        qn = _head_norm(z_ref[:, o_q + p * LANES:o_q + (p + 1) * LANES], gq_ref[...])
        kn = _head_norm(z_ref[:, o_k + p * LANES:o_k + (p + 1) * LANES], gk_ref[...])
        v = z_ref[:, o_v + p * LANES:o_v + (p + 1) * LANES]
        ko_ref[0, :, sl] = kn
        vo_ref[0, :, sl] = v
        if prompt:
            he, ho = 2 * p, 2 * p + 1
            qe, qo = _head_tiles(qn * (ATTN_SCALE * LOG2E), _aug_q(pieces, he), _aug_q(pieces, ho))
            ke, ko = _head_tiles(kn, _aug_k(pieces, he), _aug_k(pieces, ho))
            ve, vo = _head_tiles(v, 1.0, 1.0)
            qa_ref[0, he] = qe.astype(BF16)
            qa_ref[0, ho] = qo.astype(BF16)
            ka_ref[0, he] = ke.astype(BF16)
            ka_ref[0, ho] = ko.astype(BF16)
            va_ref[0, he] = ve.astype(BF16)
            va_ref[0, ho] = vo.astype(BF16)
        else:
            qo_ref[0, :, sl] = qn

    zb = jax.nn.gelu(z_ref[:, o_zb:o_zb + 2 * WIDTH_B])
    u = zb[:, :WIDTH_B]
    vbn = _rms(zb[:, WIDTH_B:]) * gv_ref[...]
    if not prompt:
        vbn_ref[0] = vbn
    vbb = vbn.astype(BF16)
    keep = _tril_ones(mix_len, mix_period)
    wm = [jnp.where(keep, wsp_ref[g], 0.0).astype(BF16) for g in range(N_GROUPS)]
    rows = []
    for c0 in range(0, tm, mix_len):
        cols = [jnp.dot(wm[g], vbb[c0:c0 + mix_len, g * GROUP_DIM:(g + 1) * GROUP_DIM],
                        preferred_element_type=F32) for g in range(N_GROUPS)]
        rows.append(jnp.concatenate(cols, axis=1) + bsp_ref[...])
    mixed = rows[0] if len(rows) == 1 else jnp.concatenate(rows, axis=0)
    b = (u * mixed).astype(BF16)
    pb = jnp.dot(b, wpb_ref[...], preferred_element_type=F32)
    mb_ref[0] = (jax.nn.sigmoid(z_ref[:, o_gb:o_gb + D_MODEL]) * pb).astype(BF16)
    gas_ref[0] = jax.nn.sigmoid(z_ref[:, o_ga:o_ga + D_MODEL]).astype(BF16)


def _mix_kernel(x_ref, shift_ref, scale_ref, g_ref, wcat_ref, bf_ref, gq_ref, gk_ref, gv_ref,
                wsp_ref, bsp_ref, wpb_ref, *rest, n_tiles, **static):
    n_z = min(n_tiles, 2)
    z_refs = rest[-n_z:]
    outs = rest[:-n_z]
    project = functools.partial(_mix_project, x_ref, shift_ref, scale_ref, g_ref, wcat_ref)
    finish = functools.partial(_mix_finish, bf_ref=bf_ref, gq_ref=gq_ref, gk_ref=gk_ref, gv_ref=gv_ref,
                               wsp_ref=wsp_ref, bsp_ref=bsp_ref, wpb_ref=wpb_ref, outs=outs, **static)
    i = pl.program_id(1)
    if n_tiles == 1:
        project(z_refs[0])
        finish(z_refs[0], True)
        return

    @pl.when(i == 0)
    def _():
        project(z_refs[0])

    for parity in range(2):
        @pl.when((i > 0) & (i < n_tiles) & (i % 2 == parity))
        def _():
            project(z_refs[parity])
            finish(z_refs[1 - parity], i == 1)

    @pl.when(i == n_tiles)
    def _():
        finish(z_refs[(n_tiles - 1) % 2], n_tiles == 1)


def _mix(x, shift, scale, g, wcat, bf, gq, gk, gv, wsp, bsp, wpb, *, tm, mix_len, mix_period, prompt):
    nb, s, _ = x.shape
    n_tiles = s // tm
    steps = n_tiles if n_tiles == 1 else n_tiles + 1
    done = lambda i: jnp.maximum(i - 1, 0) if n_tiles > 1 else i
    row = lambda w: pl.BlockSpec((1, tm, w), lambda b, i: (b, done(i), 0))
    head = pl.BlockSpec((1, N_HEADS, tm, LANES), lambda b, i: (b, 0, done(i), 0))
    x_spec = pl.BlockSpec((1, tm, D_MODEL), lambda b, i: (b, jnp.minimum(i, n_tiles - 1), 0))
    sds = jax.ShapeDtypeStruct
    tok = lambda w, dt: sds((nb, s, w), dt)
    hm = sds((nb, N_HEADS, s, LANES), BF16)
    if prompt:
        out_specs = [head, head, head, row(WIDTH_A), row(WIDTH_A), row(LANES), row(D_MODEL), row(D_MODEL)]
        out_shape = [hm, hm, hm, tok(WIDTH_A, F32), tok(WIDTH_A, F32), tok(LANES, F32),
                     tok(D_MODEL, BF16), tok(D_MODEL, BF16)]
        scratch = [pltpu.VMEM((1, LANES), F32)]
    else:
        out_specs = [row(WIDTH_A), row(WIDTH_A), row(WIDTH_A), row(LANES), row(WIDTH_B),
                     row(D_MODEL), row(D_MODEL)]
        out_shape = [tok(WIDTH_A, F32), tok(WIDTH_A, F32), tok(WIDTH_A, F32), tok(LANES, F32),
                     tok(WIDTH_B, F32), tok(D_MODEL, BF16), tok(D_MODEL, BF16)]
        scratch = []
    scratch += [pltpu.VMEM((tm, wcat.shape[1]), F32)] * min(n_tiles, 2)
    kern = functools.partial(_mix_kernel, n_tiles=n_tiles, tm=tm, mix_len=mix_len, mix_period=mix_period,
                             prompt=prompt)
    return pl.pallas_call(
        kern,
        grid=(nb, steps),
        in_specs=[x_spec, _mod_spec(shift, tm), _mod_spec(scale, tm), _const_spec((1, D_MODEL)),
                  _const_spec(wcat.shape), _const_spec(bf.shape), _const_spec(gq.shape),
                  _const_spec(gk.shape), _const_spec(gv.shape), _const_spec(wsp.shape),
                  _const_spec(bsp.shape), _const_spec(wpb.shape)],
        out_specs=out_specs,
        out_shape=out_shape,
        scratch_shapes=scratch,
        compiler_params=_params(("arbitrary", "arbitrary")),
        name="mixer_in_prompt" if prompt else "mixer_in_sample",
    )(x, shift, scale, g.reshape(1, D_MODEL), wcat, bf, gq, gk, gv, wsp, bsp, wpb)


def _scores(q, k):
    return lax.dot_general(q, k, (((1,), (1,)), ((), ())), preferred_element_type=F32)


def _softmax_update(s, v, m_old, acc_old, mask):
    if mask is not None:
        s = jnp.where(mask, s, -jnp.inf)
    m_new = jnp.maximum(m_old, jnp.max(s, axis=-1, keepdims=True))
    cols = s.shape[1]
    m_wide = pltpu.repeat(m_new, cols // LANES, axis=1) if cols >= LANES else m_new[:, :cols]
    p = jnp.exp2(s - m_wide)
    alpha = jnp.exp2(m_old - m_new)
    acc = alpha * acc_old + jnp.dot(p.astype(BF16), v, preferred_element_type=F32)
    return m_new, acc


def _pair_output(acc_even, acc_odd):
    lane = lax.broadcasted_iota(jnp.int32, (1, LANES), 1)
    oe = acc_even * pltpu.roll(1.0 / acc_even, HEAD_DIM, 1)
    oo = acc_odd * pltpu.roll(1.0 / acc_odd, HEAD_DIM, 1)
    return jnp.where(lane < HEAD_DIM, oe, pltpu.roll(oo, HEAD_DIM, 1))


def _attn_kernel(q_ref, k_ref, v_ref, o_ref, s0_ref, s1_ref, acc_ref, m_ref, *, t):
    qi = pl.program_id(2)
    m_ref[...] = jnp.full(m_ref.shape, -jnp.inf, F32)
    acc_ref[...] = jnp.zeros(acc_ref.shape, F32)

    def rows(j):
        return pl.ds(pl.multiple_of(j * t, t), t)

    def qk(j, s_ref):
        for h in range(2):
            s_ref[h] = _scores(q_ref[0, h], k_ref[0, h, rows(j), :])

    def soft_pv(j, s_ref, mask):
        for h in range(2):
            m_new, acc = _softmax_update(s_ref[h], v_ref[0, h, rows(j), :], m_ref[h], acc_ref[h], mask)
            m_ref[h] = m_new
            acc_ref[h] = acc

    def body(u, carry):
        j = 2 * u
        qk(j + 1, s1_ref)
        soft_pv(j, s0_ref, None)
        qk(j + 2, s0_ref)
        soft_pv(j + 1, s1_ref, None)
        return carry

    causal = _tril_ones(t)
    qk(0, s0_ref)
    lax.fori_loop(0, qi // 2, body, 0)

    @pl.when(qi % 2 == 1)
    def _():
        qk(qi, s1_ref)
        soft_pv(qi - 1, s0_ref, None)
        soft_pv(qi, s1_ref, causal)

    @pl.when(qi % 2 == 0)
    def _():
        soft_pv(qi, s0_ref, causal)

    o_ref[0] = _pair_output(acc_ref[0], acc_ref[1]).astype(o_ref.dtype)


def _attn(qa, ka, va, t):
    nb, _, s, _ = qa.shape
    kv = pl.BlockSpec((1, 2, s, LANES), lambda b, p, i: (b, p, 0, 0))
    return pl.pallas_call(
        functools.partial(_attn_kernel, t=t),
        grid=(nb, N_PAIRS, s // t),
        in_specs=[pl.BlockSpec((1, 2, t, LANES), lambda b, p, i: (b, p, i, 0)), kv, kv],
        out_specs=pl.BlockSpec((1, t, LANES), lambda b, p, i: (b, i, p)),
        out_shape=jax.ShapeDtypeStruct((nb, s, WIDTH_A), BF16),
        scratch_shapes=[pltpu.VMEM((2, t, t), F32), pltpu.VMEM((2, t, t), F32),
                        pltpu.VMEM((2, t, LANES), F32), pltpu.VMEM((2, t, LANES), F32)],
        compiler_params=_params(("arbitrary", "arbitrary", "arbitrary")),
        name="fox_prompt_attention",
    )(qa, ka, va)


def _attn_sample_kernel(q_ref, k_ref, v_ref, lf_ref, kc_ref, vc_ref, lfc_ref, o_ref, *, t_new):
    c_cache = _cumsum_rows(lfc_ref[0])
    p_len = c_cache.shape[0]
    c_new = _cumsum_rows(lf_ref[0]) + c_cache[p_len - 1:p_len, :]
    pc_cache = _split3(c_cache * LOG2E)
    pc_new = _split3(c_new * LOG2E)
    causal = _tril_ones(t_new)
    for p in range(N_PAIRS):
        sl = slice(p * LANES, (p + 1) * LANES)
        he, ho = 2 * p, 2 * p + 1
        q_t = _head_tiles(q_ref[0, :, sl] * (ATTN_SCALE * LOG2E), _aug_q(pc_new, he), _aug_q(pc_new, ho))
        kc_t = _head_tiles(kc_ref[0, :, sl], _aug_k(pc_cache, he), _aug_k(pc_cache, ho))
        kn_t = _head_tiles(k_ref[0, :, sl], _aug_k(pc_new, he), _aug_k(pc_new, ho))
        vc_t = _head_tiles(vc_ref[0, :, sl], 1.0, 1.0)
        vn_t = _head_tiles(v_ref[0, :, sl], 1.0, 1.0)
        accs = []
        for j in range(2):
            q = q_t[j].astype(BF16)
            m0 = jnp.full((t_new, LANES), -jnp.inf, F32)
            a0 = jnp.zeros((t_new, LANES), F32)
            m1, a1 = _softmax_update(_scores(q, kc_t[j].astype(BF16)), vc_t[j].astype(BF16), m0, a0, None)
            _, a2 = _softmax_update(_scores(q, kn_t[j].astype(BF16)), vn_t[j].astype(BF16), m1, a1, causal)
            accs.append(a2)
        o_ref[0, :, sl] = _pair_output(accs[0], accs[1]).astype(o_ref.dtype)


def _attn_sample(q, k, v, lf, kc, vc, lfc):
    nb, t_new, _ = q.shape
    p_len = kc.shape[1]
    new = lambda w: pl.BlockSpec((1, t_new, w), lambda b: (b, 0, 0))
    old = lambda w: pl.BlockSpec((1, p_len, w), lambda b: (b, 0, 0))
    return pl.pallas_call(
        functools.partial(_attn_sample_kernel, t_new=t_new),
        grid=(nb,),
        in_specs=[new(WIDTH_A), new(WIDTH_A), new(WIDTH_A), new(LANES), old(WIDTH_A), old(WIDTH_A), old(LANES)],
        out_specs=new(WIDTH_A),
        out_shape=jax.ShapeDtypeStruct((nb, t_new, WIDTH_A), BF16),
        compiler_params=_params(("arbitrary",)),
        name="fox_sample_attention",
    )(q, k, v, lf, kc, vc, lfc)


def _merge_kernel(a_ref, gas_ref, mb_ref, x_ref, gate_ref, wpa_ref, wo_ref, o_ref):
    pa = jnp.dot(a_ref[0], wpa_ref[...], preferred_element_type=F32)
    m = gas_ref[0].astype(F32) * pa + mb_ref[0].astype(F32)
    y = jnp.dot(m.astype(BF16), wo_ref[...], preferred_element_type=F32)
    o_ref[0] = x_ref[0] + gate_ref[0] * y


def _merge(a, gas, mb, x, gate, wpa, wo, tm):
    nb, s, _ = x.shape
    row = lambda w: pl.BlockSpec((1, tm, w), lambda b, i: (b, i, 0))
    return pl.pallas_call(
        _merge_kernel,
        grid=(nb, s // tm),
        in_specs=[row(WIDTH_A), row(D_MODEL), row(D_MODEL), row(D_MODEL), _mod_spec(gate, tm),
                  _const_spec(wpa.shape), _const_spec(wo.shape)],
        out_specs=row(D_MODEL),
        out_shape=jax.ShapeDtypeStruct(x.shape, F32),
        compiler_params=_params(("arbitrary", "arbitrary")),
        name="branch_merge",
    )(a, gas, mb, x, gate, wpa, wo)


def _pad_lanes(v, fill=0.0):
    v = v.reshape(1, -1)
    return jnp.pad(v, ((0, 0), (0, LANES - v.shape[1])), constant_values=fill)


def _layer(xp, xs, mp, ms, cache_k, cache_v, cache_lf, g1, wg1, wu1, wd1, gm, w_in, b_forget, g_q, g_k, g_v,
           w_spatial, b_spatial, w_proj_a, w_proj_b, w_out, g2, wg2, wu2, wd2, *, tm, t_attn):
    nb_s, t_new, _ = xs.shape
    rows_s = nb_s * t_new
    xs = xs.reshape(1, rows_s, D_MODEL)
    ms = [jnp.repeat(m, t_new, axis=0).reshape(1, rows_s, D_MODEL) for m in ms]
    mp = [m[:, None, :] for m in mp]

    wgu1 = jnp.concatenate([wg1, wu1], axis=1).astype(BF16)
    wgu2 = jnp.concatenate([wg2, wu2], axis=1).astype(BF16)
    wd1 = wd1.astype(BF16)
    wd2 = wd2.astype(BF16)
    o_f = 3 * WIDTH_A
    o_zb = o_f + N_HEADS
    wf = jnp.pad(w_in[:, o_f:o_zb], ((0, 0), (0, LANES - N_HEADS)))
    wcat = jnp.concatenate([w_in[:, :o_f], w_in[:, o_zb:], wf], axis=1).astype(BF16)
    bf = _pad_lanes(b_forget)
    gq = jnp.tile(g_q, 2).reshape(1, LANES)
    gk = jnp.tile(g_k, 2).reshape(1, LANES)
    gv = g_v.reshape(1, WIDTH_B)
    bsp = jnp.repeat(b_spatial.T, GROUP_DIM, axis=1)
    wpa = w_proj_a.astype(BF16)
    wpb = w_proj_b.astype(BF16)
    wo = w_out.astype(BF16)

    xp = _ffn(xp, mp[0], mp[1], mp[2], g1, wgu1, wd1, tm)
    xs = _ffn(xs, ms[0], ms[1], ms[2], g1, wgu1, wd1, rows_s)

    qa, ka, va, k_p, v_p, lf_p, gas, mb = _mix(
        xp, mp[3], mp[4], gm, wcat, bf, gq, gk, gv, w_spatial, bsp, wpb,
        tm=tm, mix_len=GMLP_CHUNK, mix_period=GMLP_CHUNK, prompt=True)
    a = _attn(qa, ka, va, t_attn)
    xp = _merge(a, gas, mb, xp, mp[5], wpa, wo, tm)

    wsp_s = jnp.tile(w_spatial[:, :t_new, :t_new], (1, nb_s, nb_s))
    bsp_s = jnp.tile(bsp[:t_new], (nb_s, 1))
    q_s, k_s, v_s, lf_s, vbn_s, gas_s, mb_s = _mix(
        xs, ms[3], ms[4], gm, wcat, bf, gq, gk, gv, wsp_s, bsp_s, wpb,
        tm=rows_s, mix_len=rows_s, mix_period=t_new, prompt=False)
    p_len = cache_k.shape[1]
    per_req = lambda z: z.reshape(nb_s, t_new, z.shape[-1])
    lfc = jnp.pad(cache_lf, ((0, 0), (0, 0), (0, LANES - N_HEADS)))
    a_s = _attn_sample(per_req(q_s), per_req(k_s), per_req(v_s), per_req(lf_s),
                       cache_k.reshape(nb_s, p_len, WIDTH_A), cache_v.reshape(nb_s, p_len, WIDTH_A), lfc)
    xs = _merge(a_s.reshape(1, rows_s, WIDTH_A), gas_s, mb_s, xs, ms[5], wpa, wo, rows_s)

    xp = _ffn(xp, mp[6], mp[7], mp[8], g2, wgu2, wd2, tm)
    xs = _ffn(xs, ms[6], ms[7], ms[8], g2, wgu2, wd2, rows_s)

    nb_p, s_p, _ = xp.shape
    heads = lambda z, nb, s: z.reshape(nb, s, N_HEADS, HEAD_DIM)
    outs = (heads(k_p, nb_p, s_p), heads(v_p, nb_p, s_p), lf_p[:, :, :N_HEADS],
            heads(per_req(k_s), nb_s, t_new), heads(per_req(v_s), nb_s, t_new),
            per_req(lf_s)[:, :, :N_HEADS], per_req(vbn_s))
    return xp, xs.reshape(nb_s, t_new, D_MODEL), outs


def kernel(x_prompt, x_sample, c_prompt, c_sample, cache_fox_k, cache_fox_v, cache_fox_logf, w_ada, b_ada, g_norm_ffn1, w_ffn1_gate, w_ffn1_up, w_ffn1_down, g_norm_mix, w_in, b_forget, g_q, g_k, g_gmlp_v, w_spatial, b_spatial, w_proj_a, w_proj_b, w_out, g_norm_ffn2, w_ffn2_gate, w_ffn2_up, w_ffn2_down):
    depth = w_ada.shape[0]
    nb_p, s_p, _ = x_prompt.shape
    nb_s = x_sample.shape[0]
    tm = min(512, s_p)
    t_attn = min(512, s_p)
    xp, xs = x_prompt, x_sample
    c_all = jnp.concatenate([c_prompt, c_sample], axis=0)
    rows = c_all.shape[0]
    c_all = jnp.pad(c_all, ((0, -rows % 8), (0, 0)))
    stacks = [[] for _ in range(7)]
    for l in range(depth):
        mod = _ada(c_all, w_ada[l], b_ada[l])
        mp = [mod[:nb_p, i * D_MODEL:(i + 1) * D_MODEL] for i in range(N_MOD)]
        ms = [mod[nb_p:nb_p + nb_s, i * D_MODEL:(i + 1) * D_MODEL] for i in range(N_MOD)]
        xp, xs, outs = _layer(
            xp, xs, mp, ms, cache_fox_k[l], cache_fox_v[l], cache_fox_logf[l],
            g_norm_ffn1[l], w_ffn1_gate[l], w_ffn1_up[l], w_ffn1_down[l],
            g_norm_mix[l], w_in[l], b_forget[l], g_q[l], g_k[l], g_gmlp_v[l], w_spatial[l], b_spatial[l],
            w_proj_a[l], w_proj_b[l], w_out[l],
            g_norm_ffn2[l], w_ffn2_gate[l], w_ffn2_up[l], w_ffn2_down[l], tm=tm, t_attn=t_attn)
        for st, o in zip(stacks, outs):
            st.append(o)
    return (xp, xs) + tuple(jnp.stack(st) for st in stacks)
```

```python
import functools

import jax
import jax.numpy as jnp
from jax import lax
from jax.experimental import pallas as pl
from jax.experimental.pallas import tpu as pltpu

D_MODEL = 1024
N_HEADS = 8
HEAD_DIM = 64
WIDTH_A = N_HEADS * HEAD_DIM
N_GROUPS = 4
GROUP_DIM = 128
WIDTH_B = N_GROUPS * GROUP_DIM
GMLP_CHUNK = 128
D_FF = 2816
N_MOD = 9
EPS = 1e-6
ATTN_SCALE = HEAD_DIM ** -0.5
LOG2E = 1.4426950408889634

LANES = 128
SUBLANES = 8
N_PAIRS = N_HEADS // 2
VMEM_LIMIT = 56 * 1024 * 1024

F32 = jnp.float32
BF16 = jnp.bfloat16


def _const_spec(shape):
    nd = len(shape)
    return pl.BlockSpec(shape, lambda *_: (0,) * nd, pipeline_mode=pl.Buffered(1))


def _params(sem):
    return pltpu.CompilerParams(dimension_semantics=sem, vmem_limit_bytes=VMEM_LIMIT)


def _rms(x):
    return x * lax.rsqrt(jnp.mean(x * x, axis=-1, keepdims=True) + EPS)


def _split3(c):
    hi = c.astype(BF16).astype(F32)
    r = c - hi
    mid = r.astype(BF16).astype(F32)
    lo = (r - mid).astype(BF16).astype(F32)
    return hi, mid, lo


def _tril_ones(n, period=None):
    row = lax.broadcasted_iota(jnp.int32, (n, n), 0)
    col = lax.broadcasted_iota(jnp.int32, (n, n), 1)
    keep = col <= row
    if period is not None and period != n:
        keep = keep & ((row // period) == (col // period))
    return keep


def _cumsum_rows(x):
    n = x.shape[0]
    groups = n // SUBLANES
    x3 = x.reshape(groups, SUBLANES, LANES)
    sub = lax.broadcasted_iota(jnp.int32, (1, SUBLANES, LANES), 1)
    k = 1
    while k < SUBLANES:
        x3 = x3 + jnp.where(sub >= k, pltpu.roll(x3, k, 1), 0.0)
        k *= 2
    tot = x3[:, SUBLANES - 1:SUBLANES, :]
    k = 1
    while k < groups:
        tot = tot + jnp.concatenate([jnp.zeros((k, 1, LANES), F32), tot[:groups - k]], axis=0)
        k *= 2
    before = jnp.concatenate([jnp.zeros((1, 1, LANES), F32), tot[:groups - 1]], axis=0)
    return (x3 + before).reshape(n, LANES)


def _head_norm(slab, gain):
    lane = lax.broadcasted_iota(jnp.int32, (1, LANES), 1)
    lo = lane < HEAD_DIM
    sq = slab * slab
    ss_lo = jnp.sum(jnp.where(lo, sq, 0.0), axis=-1, keepdims=True)
    ss_hi = jnp.sum(jnp.where(lo, 0.0, sq), axis=-1, keepdims=True)
    r = jnp.where(lo, lax.rsqrt(ss_lo / HEAD_DIM + EPS), lax.rsqrt(ss_hi / HEAD_DIM + EPS))
    return slab * r * gain


def _head_tiles(slab, fill_even, fill_odd):
    lane = lax.broadcasted_iota(jnp.int32, (1, LANES), 1)
    lo = lane < HEAD_DIM
    even = jnp.where(lo, slab, fill_even)
    odd = jnp.where(lo, pltpu.roll(slab, HEAD_DIM, 1), fill_odd)
    return even, odd


def _aug_q(pieces, h):
    lane = lax.broadcasted_iota(jnp.int32, (1, LANES), 1)
    hi, mid, lo = (p[:, h:h + 1] for p in pieces)
    ones = jnp.where((lane >= HEAD_DIM + 3) & (lane < HEAD_DIM + 6), 1.0, 0.0)
    return jnp.where(lane == HEAD_DIM, hi, jnp.where(lane == HEAD_DIM + 1, mid,
                     jnp.where(lane == HEAD_DIM + 2, lo, ones)))


def _aug_k(pieces, h):
    lane = lax.broadcasted_iota(jnp.int32, (1, LANES), 1)
    hi, mid, lo = (-p[:, h:h + 1] for p in pieces)
    ones = jnp.where((lane >= HEAD_DIM) & (lane < HEAD_DIM + 3), 1.0, 0.0)
    return jnp.where(lane == HEAD_DIM + 3, hi, jnp.where(lane == HEAD_DIM + 4, mid,
                     jnp.where(lane == HEAD_DIM + 5, lo, ones)))


def _log_sigmoid(x):
    return jnp.minimum(x, 0.0) - jnp.log1p(jnp.exp(-jnp.abs(x)))


def _ada_kernel(c_ref, w_ref, b_ref, o_ref):
    h = jax.nn.silu(c_ref[...]).astype(BF16)
    o_ref[...] = jnp.dot(h, w_ref[...].astype(BF16), preferred_element_type=F32) + b_ref[...]


def _ada(c, w_ada, b_ada):
    rows = c.shape[0]
    n = w_ada.shape[1]
    tn = D_MODEL
    return pl.pallas_call(
        _ada_kernel,
        grid=(n // tn,),
        in_specs=[pl.BlockSpec((rows, D_MODEL), lambda j: (0, 0)),
                  pl.BlockSpec((D_MODEL, tn), lambda j: (0, j)),
                  pl.BlockSpec((1, tn), lambda j: (0, j))],
        out_specs=pl.BlockSpec((rows, tn), lambda j: (0, j)),
        out_shape=jax.ShapeDtypeStruct((rows, n), F32),
        compiler_params=_params(("arbitrary",)),
        name="ada_modulation",
    )(c, w_ada, b_ada.reshape(1, n))


def _ffn_kernel(x_ref, shift_ref, scale_ref, gate_ref, g_ref, wg_ref, wu_ref, wd_ref, o_ref):
    x = x_ref[0]
    h = ((_rms(x) * g_ref[...]) * (1.0 + scale_ref[0]) + shift_ref[0]).astype(BF16)
    a = jax.nn.silu(jnp.dot(h, wg_ref[...], preferred_element_type=F32))
    a = a * jnp.dot(h, wu_ref[...], preferred_element_type=F32)
    y = jnp.dot(a.astype(BF16), wd_ref[...], preferred_element_type=F32)
    o_ref[0] = x + 0.5 * gate_ref[0] * y


def _mod_spec(mod, tm):
    if mod.shape[1] == 1:
        return pl.BlockSpec((1, 1, D_MODEL), lambda b, i: (b, 0, 0))
    return pl.BlockSpec((1, tm, D_MODEL), lambda b, i: (b, i, 0))


def _ffn(x, shift, scale, gate, g, wg, wu, wd, tm):
    nb, s, _ = x.shape
    row = pl.BlockSpec((1, tm, D_MODEL), lambda b, i: (b, i, 0))
    return pl.pallas_call(
        _ffn_kernel,
        grid=(nb, s // tm),
        in_specs=[row, _mod_spec(shift, tm), _mod_spec(scale, tm), _mod_spec(gate, tm),
                  _const_spec((1, D_MODEL)), _const_spec(wg.shape), _const_spec(wu.shape), _const_spec(wd.shape)],
        out_specs=row,
        out_shape=jax.ShapeDtypeStruct(x.shape, F32),
        compiler_params=_params(("arbitrary", "arbitrary")),
        name="ffn_half_step",
    )(x, shift, scale, gate, g.reshape(1, D_MODEL), wg, wu, wd)


def _mix_project(x_ref, shift_ref, scale_ref, g_ref, wcat_ref, z_ref):
    n = (_rms(x_ref[0]) * g_ref[...]) * (1.0 + scale_ref[0]) + shift_ref[0]
    z_ref[...] = jnp.dot(n.astype(BF16), wcat_ref[...], preferred_element_type=F32)


def _mix_finish(z_ref, first_tile, bf_ref, gq_ref, gk_ref, gv_ref, wsp_ref, bsp_ref, wpb_ref, outs,
                *, tm, mix_len, mix_period, prompt):
    if prompt:
        qa_ref, ka_ref, va_ref, ko_ref, vo_ref, lf_ref, gas_ref, mb_ref, carry_ref = outs
    else:
        qo_ref, ko_ref, vo_ref, lf_ref, vbn_ref, gas_ref, mb_ref = outs
    o_q, o_k, o_v = 0, WIDTH_A, 2 * WIDTH_A
    o_zb = 3 * WIDTH_A
    o_ga = o_zb + 2 * WIDTH_B
    o_gb = o_ga + D_MODEL
    o_f = o_gb + D_MODEL

    lane = lax.broadcasted_iota(jnp.int32, (1, LANES), 1)
    logf = jnp.where(lane < N_HEADS, _log_sigmoid(z_ref[:, o_f:o_f + LANES] + bf_ref[...]), 0.0)
    lf_ref[0] = logf

    if prompt:
        c = _cumsum_rows(logf) + jnp.where(first_tile, 0.0, carry_ref[...])
        carry_ref[...] = c[tm - 1:tm, :]
        pieces = _split3(c * LOG2E)

    for p in range(N_PAIRS):
        sl = slice(p * LANES, (p + 1) * LANES)
        qn = _head_norm(z_ref[:, o_q + p * LANES:o_q + (p + 1) * LANES], gq_ref[...])
        kn = _head_norm(z_ref[:, o_k + p * LANES:o_k + (p + 1) * LANES], gk_ref[...])
        v = z_ref[:, o_v + p * LANES:o_v + (p + 1) * LANES]
        ko_ref[0, :, sl] = kn
        vo_ref[0, :, sl] = v
        if prompt:
            he, ho = 2 * p, 2 * p + 1
            qe, qo = _head_tiles(qn * (ATTN_SCALE * LOG2E), _aug_q(pieces, he), _aug_q(pieces, ho))
            ke, ko = _head_tiles(kn, _aug_k(pieces, he), _aug_k(pieces, ho))
            ve, vo = _head_tiles(v, 1.0, 1.0)
            qa_ref[0, he] = qe.astype(BF16)
            qa_ref[0, ho] = qo.astype(BF16)
            ka_ref[0, he] = ke.astype(BF16)
            ka_ref[0, ho] = ko.astype(BF16)
            va_ref[0, he] = ve.astype(BF16)
            va_ref[0, ho] = vo.astype(BF16)
        else:
            qo_ref[0, :, sl] = qn

    zb = jax.nn.gelu(z_ref[:, o_zb:o_zb + 2 * WIDTH_B])
    u = zb[:, :WIDTH_B]
    vbn = _rms(zb[:, WIDTH_B:]) * gv_ref[...]
    if not prompt:
        vbn_ref[0] = vbn
    vbb = vbn.astype(BF16)
    keep = _tril_ones(mix_len, mix_period)
    wm = [jnp.where(keep, wsp_ref[g], 0.0).astype(BF16) for g in range(N_GROUPS)]
    rows = []
    for c0 in range(0, tm, mix_len):
        cols = [jnp.dot(wm[g], vbb[c0:c0 + mix_len, g * GROUP_DIM:(g + 1) * GROUP_DIM],
                        preferred_element_type=F32) for g in range(N_GROUPS)]
        rows.append(jnp.concatenate(cols, axis=1) + bsp_ref[...])
    mixed = rows[0] if len(rows) == 1 else jnp.concatenate(rows, axis=0)
    b = (u * mixed).astype(BF16)
    pb = jnp.dot(b, wpb_ref[...], preferred_element_type=F32)
    mb_ref[0] = (jax.nn.sigmoid(z_ref[:, o_gb:o_gb + D_MODEL]) * pb).astype(BF16)
    gas_ref[0] = jax.nn.sigmoid(z_ref[:, o_ga:o_ga + D_MODEL]).astype(BF16)


def _mix_kernel(x_ref, shift_ref, scale_ref, g_ref, wcat_ref, bf_ref, gq_ref, gk_ref, gv_ref,
                wsp_ref, bsp_ref, wpb_ref, *rest, n_tiles, **static):
    n_z = min(n_tiles, 2)
    z_refs = rest[-n_z:]
    outs = rest[:-n_z]
    project = functools.partial(_mix_project, x_ref, shift_ref, scale_ref, g_ref, wcat_ref)
    finish = functools.partial(_mix_finish, bf_ref=bf_ref, gq_ref=gq_ref, gk_ref=gk_ref, gv_ref=gv_ref,
                               wsp_ref=wsp_ref, bsp_ref=bsp_ref, wpb_ref=wpb_ref, outs=outs, **static)
    i = pl.program_id(1)
    if n_tiles == 1:
        project(z_refs[0])
        finish(z_refs[0], True)
        return

    @pl.when(i == 0)
    def _():
        project(z_refs[0])

    for parity in range(2):
        @pl.when((i > 0) & (i < n_tiles) & (i % 2 == parity))
        def _():
            project(z_refs[parity])
            finish(z_refs[1 - parity], i == 1)

    @pl.when(i == n_tiles)
    def _():
        finish(z_refs[(n_tiles - 1) % 2], n_tiles == 1)


def _mix(x, shift, scale, g, wcat, bf, gq, gk, gv, wsp, bsp, wpb, *, tm, mix_len, mix_period, prompt):
    nb, s, _ = x.shape
    n_tiles = s // tm
    steps = n_tiles if n_tiles == 1 else n_tiles + 1
    done = lambda i: jnp.maximum(i - 1, 0) if n_tiles > 1 else i
    row = lambda w: pl.BlockSpec((1, tm, w), lambda b, i: (b, done(i), 0))
    head = pl.BlockSpec((1, N_HEADS, tm, LANES), lambda b, i: (b, 0, done(i), 0))
    x_spec = pl.BlockSpec((1, tm, D_MODEL), lambda b, i: (b, jnp.minimum(i, n_tiles - 1), 0))
    sds = jax.ShapeDtypeStruct
    tok = lambda w, dt: sds((nb, s, w), dt)
    hm = sds((nb, N_HEADS, s, LANES), BF16)
    if prompt:
        out_specs = [head, head, head, row(WIDTH_A), row(WIDTH_A), row(LANES), row(D_MODEL), row(D_MODEL)]
        out_shape = [hm, hm, hm, tok(WIDTH_A, F32), tok(WIDTH_A, F32), tok(LANES, F32),
                     tok(D_MODEL, BF16), tok(D_MODEL, BF16)]
        scratch = [pltpu.VMEM((1, LANES), F32)]
    else:
        out_specs = [row(WIDTH_A), row(WIDTH_A), row(WIDTH_A), row(LANES), row(WIDTH_B),
                     row(D_MODEL), row(D_MODEL)]
        out_shape = [tok(WIDTH_A, F32), tok(WIDTH_A, F32), tok(WIDTH_A, F32), tok(LANES, F32),
                     tok(WIDTH_B, F32), tok(D_MODEL, BF16), tok(D_MODEL, BF16)]
        scratch = []
    scratch += [pltpu.VMEM((tm, wcat.shape[1]), F32)] * min(n_tiles, 2)
    kern = functools.partial(_mix_kernel, n_tiles=n_tiles, tm=tm, mix_len=mix_len, mix_period=mix_period,
                             prompt=prompt)
    return pl.pallas_call(
        kern,
        grid=(nb, steps),
        in_specs=[x_spec, _mod_spec(shift, tm), _mod_spec(scale, tm), _const_spec((1, D_MODEL)),
                  _const_spec(wcat.shape), _const_spec(bf.shape), _const_spec(gq.shape),
                  _const_spec(gk.shape), _const_spec(gv.shape), _const_spec(wsp.shape),
                  _const_spec(bsp.shape), _const_spec(wpb.shape)],
        out_specs=out_specs,
        out_shape=out_shape,
        scratch_shapes=scratch,
        compiler_params=_params(("arbitrary", "arbitrary")),
        name="mixer_in_prompt" if prompt else "mixer_in_sample",
    )(x, shift, scale, g.reshape(1, D_MODEL), wcat, bf, gq, gk, gv, wsp, bsp, wpb)


def _scores(q, k):
    return lax.dot_general(q, k, (((1,), (1,)), ((), ())), preferred_element_type=F32)


def _softmax_update(s, v, m_old, acc_old, mask):
    if mask is not None:
        s = jnp.where(mask, s, -jnp.inf)
    m_new = jnp.maximum(m_old, jnp.max(s, axis=-1, keepdims=True))
    cols = s.shape[1]
    m_wide = jnp.concatenate([m_new] * (cols // LANES), axis=1) if cols >= LANES else m_new[:, :cols]
    p = jnp.exp2(s - m_wide)
    alpha = jnp.exp2(m_old - m_new)
    acc = alpha * acc_old + jnp.dot(p.astype(BF16), v, preferred_element_type=F32)
    return m_new, acc


def _pair_output(acc_even, acc_odd):
    lane = lax.broadcasted_iota(jnp.int32, (1, LANES), 1)
    oe = acc_even * pltpu.roll(1.0 / acc_even, HEAD_DIM, 1)
    oo = acc_odd * pltpu.roll(1.0 / acc_odd, HEAD_DIM, 1)
    return jnp.where(lane < HEAD_DIM, oe, pltpu.roll(oo, HEAD_DIM, 1))


def _attn_kernel(q_ref, qn_ref, k_ref, v_ref, o_ref, s0_ref, s1_ref, acc_ref, m_ref, *, tq, tk):
    qi = pl.program_id(2)
    m_ref[...] = jnp.full(m_ref.shape, -jnp.inf, F32)
    acc_ref[...] = jnp.zeros(acc_ref.shape, F32)

    def keys(j):
        return pl.ds(pl.multiple_of(j * tk, tk), tk)

    def qk(src_ref, j, s_ref, r0=0):
        for h in range(2):
            s_ref[h, r0:, :] = _scores(src_ref[0, h, r0:, :], k_ref[0, h, keys(j), :])

    def soft_pv(j, s_ref, mask, r0=0):
        for h in range(2):
            m_new, acc = _softmax_update(s_ref[h, r0:, :], v_ref[0, h, keys(j), :],
                                         m_ref[h, r0:, :], acc_ref[h, r0:, :], mask)
            m_ref[h, r0:, :] = m_new
            acc_ref[h, r0:, :] = acc

    @pl.when(qi == 0)
    def _():
        qk(q_ref, 0, s0_ref)

    def body(u, carry):
        j = 2 * u
        qk(q_ref, j + 1, s1_ref)
        soft_pv(j, s0_ref, None)
        qk(q_ref, j + 2, s0_ref)
        soft_pv(j + 1, s1_ref, None)
        return carry

    lax.fori_loop(0, qi, body, 0)

    row = lax.broadcasted_iota(jnp.int32, (tq, tk), 0)
    col = lax.broadcasted_iota(jnp.int32, (tq, tk), 1)
    qk(q_ref, 2 * qi + 1, s1_ref, tk)
    soft_pv(2 * qi, s0_ref, col <= row)
    qk(qn_ref, 0, s0_ref)
    soft_pv(2 * qi + 1, s1_ref, _tril_ones(tk), tk)

    o_ref[0] = _pair_output(acc_ref[0], acc_ref[1]).astype(o_ref.dtype)


def _attn(qa, ka, va, tk):
    nb, _, s, _ = qa.shape
    tq = 2 * tk
    nq = s // tq
    kv = pl.BlockSpec((1, 2, s, LANES), lambda b, p, i: (b, p, 0, 0))
    return pl.pallas_call(
        functools.partial(_attn_kernel, tq=tq, tk=tk),
        grid=(nb, N_PAIRS, nq),
        in_specs=[pl.BlockSpec((1, 2, tq, LANES), lambda b, p, i: (b, p, i, 0)),
                  pl.BlockSpec((1, 2, tq, LANES), lambda b, p, i: (b, p, jnp.minimum(i + 1, nq - 1), 0)),
                  kv, kv],
        out_specs=pl.BlockSpec((1, tq, LANES), lambda b, p, i: (b, i, p)),
        out_shape=jax.ShapeDtypeStruct((nb, s, WIDTH_A), BF16),
        scratch_shapes=[pltpu.VMEM((2, tq, tk), F32), pltpu.VMEM((2, tq, tk), F32),
                        pltpu.VMEM((2, tq, LANES), F32), pltpu.VMEM((2, tq, LANES), F32)],
        compiler_params=_params(("arbitrary", "arbitrary", "arbitrary")),
        name="fox_prompt_attention",
    )(qa, qa, ka, va)


def _attn_sample_kernel(q_ref, k_ref, v_ref, lf_ref, kc_ref, vc_ref, lfc_ref, o_ref, *, t_new):
    c_cache = _cumsum_rows(lfc_ref[0])
    p_len = c_cache.shape[0]
    c_new = _cumsum_rows(lf_ref[0]) + c_cache[p_len - 1:p_len, :]
    pc_cache = _split3(c_cache * LOG2E)
    pc_new = _split3(c_new * LOG2E)
    causal = _tril_ones(t_new)
    for p in range(N_PAIRS):
        sl = slice(p * LANES, (p + 1) * LANES)
        he, ho = 2 * p, 2 * p + 1
        q_t = _head_tiles(q_ref[0, :, sl] * (ATTN_SCALE * LOG2E), _aug_q(pc_new, he), _aug_q(pc_new, ho))
        kc_t = _head_tiles(kc_ref[0, :, sl], _aug_k(pc_cache, he), _aug_k(pc_cache, ho))
        kn_t = _head_tiles(k_ref[0, :, sl], _aug_k(pc_new, he), _aug_k(pc_new, ho))
        vc_t = _head_tiles(vc_ref[0, :, sl], 1.0, 1.0)
        vn_t = _head_tiles(v_ref[0, :, sl], 1.0, 1.0)
        accs = []
        for j in range(2):
            q = q_t[j].astype(BF16)
            m0 = jnp.full((t_new, LANES), -jnp.inf, F32)
            a0 = jnp.zeros((t_new, LANES), F32)
            m1, a1 = _softmax_update(_scores(q, kc_t[j].astype(BF16)), vc_t[j].astype(BF16), m0, a0, None)
            _, a2 = _softmax_update(_scores(q, kn_t[j].astype(BF16)), vn_t[j].astype(BF16), m1, a1, causal)
            accs.append(a2)
        o_ref[0, :, sl] = _pair_output(accs[0], accs[1]).astype(o_ref.dtype)


def _attn_sample(q, k, v, lf, kc, vc, lfc):
    nb, t_new, _ = q.shape
    p_len = kc.shape[1]
    new = lambda w: pl.BlockSpec((1, t_new, w), lambda b: (b, 0, 0))
    old = lambda w: pl.BlockSpec((1, p_len, w), lambda b: (b, 0, 0))
    return pl.pallas_call(
        functools.partial(_attn_sample_kernel, t_new=t_new),
        grid=(nb,),
        in_specs=[new(WIDTH_A), new(WIDTH_A), new(WIDTH_A), new(LANES), old(WIDTH_A), old(WIDTH_A), old(LANES)],
        out_specs=new(WIDTH_A),
        out_shape=jax.ShapeDtypeStruct((nb, t_new, WIDTH_A), BF16),
        compiler_params=_params(("arbitrary",)),
        name="fox_sample_attention",
    )(q, k, v, lf, kc, vc, lfc)


def _merge_kernel(a_ref, gas_ref, mb_ref, x_ref, gate_ref, wpa_ref, wo_ref, o_ref, *, parts):
    rows = a_ref.shape[1] // parts
    for r in range(parts):
        sl = slice(r * rows, (r + 1) * rows)
        gate = gate_ref[0] if gate_ref.shape[1] == 1 else gate_ref[0, sl]
        pa = jnp.dot(a_ref[0, sl], wpa_ref[...], preferred_element_type=F32)
        m = gas_ref[0, sl].astype(F32) * pa + mb_ref[0, sl].astype(F32)
        y = jnp.dot(m.astype(BF16), wo_ref[...], preferred_element_type=F32)
        o_ref[0, sl] = x_ref[0, sl] + gate * y


def _merge(a, gas, mb, x, gate, wpa, wo, tm):
    nb, s, _ = x.shape
    row = lambda w: pl.BlockSpec((1, tm, w), lambda b, i: (b, i, 0))
    return pl.pallas_call(
        functools.partial(_merge_kernel, parts=2 if tm % 512 == 0 else 1),
        grid=(nb, s // tm),
        in_specs=[row(WIDTH_A), row(D_MODEL), row(D_MODEL), row(D_MODEL), _mod_spec(gate, tm),
                  _const_spec(wpa.shape), _const_spec(wo.shape)],
        out_specs=row(D_MODEL),
        out_shape=jax.ShapeDtypeStruct(x.shape, F32),
        compiler_params=_params(("arbitrary", "arbitrary")),
        name="branch_merge",
    )(a, gas, mb, x, gate, wpa, wo)


def _pad_lanes(v, fill=0.0):
    v = v.reshape(1, -1)
    return jnp.pad(v, ((0, 0), (0, LANES - v.shape[1])), constant_values=fill)


def _layer(xp, xs, mp, ms, cache_k, cache_v, cache_lf, g1, wg1, wu1, wd1, gm, w_in, b_forget, g_q, g_k, g_v,
           w_spatial, b_spatial, w_proj_a, w_proj_b, w_out, g2, wg2, wu2, wd2, *, tm, t_attn):
    nb_s, t_new, _ = xs.shape
    rows_s = nb_s * t_new
    xs = xs.reshape(1, rows_s, D_MODEL)
    ms = [jnp.repeat(m, t_new, axis=0).reshape(1, rows_s, D_MODEL) for m in ms]
    mp = [m[:, None, :] for m in mp]

    wg1, wu1, wd1 = wg1.astype(BF16), wu1.astype(BF16), wd1.astype(BF16)
    wg2, wu2, wd2 = wg2.astype(BF16), wu2.astype(BF16), wd2.astype(BF16)
    o_f = 3 * WIDTH_A
    o_zb = o_f + N_HEADS
    wf = jnp.pad(w_in[:, o_f:o_zb], ((0, 0), (0, LANES - N_HEADS)))
    wcat = jnp.concatenate([w_in[:, :o_f], w_in[:, o_zb:], wf], axis=1).astype(BF16)
    bf = _pad_lanes(b_forget)
    gq = jnp.tile(g_q, 2).reshape(1, LANES)
    gk = jnp.tile(g_k, 2).reshape(1, LANES)
    gv = g_v.reshape(1, WIDTH_B)
    bsp = jnp.repeat(b_spatial.T, GROUP_DIM, axis=1)
    wpa = w_proj_a.astype(BF16)
    wpb = w_proj_b.astype(BF16)
    wo = w_out.astype(BF16)

    xp = _ffn(xp, mp[0], mp[1], mp[2], g1, wg1, wu1, wd1, tm)
    xs = _ffn(xs, ms[0], ms[1], ms[2], g1, wg1, wu1, wd1, rows_s)

    qa, ka, va, k_p, v_p, lf_p, gas, mb = _mix(
        xp, mp[3], mp[4], gm, wcat, bf, gq, gk, gv, w_spatial, bsp, wpb,
        tm=tm, mix_len=GMLP_CHUNK, mix_period=GMLP_CHUNK, prompt=True)
    a = _attn(qa, ka, va, t_attn)
    xp = _merge(a, gas, mb, xp, mp[5], wpa, wo, min(2 * tm, xp.shape[1]))

    wsp_s = jnp.tile(w_spatial[:, :t_new, :t_new], (1, nb_s, nb_s))
    bsp_s = jnp.tile(bsp[:t_new], (nb_s, 1))
    q_s, k_s, v_s, lf_s, vbn_s, gas_s, mb_s = _mix(
        xs, ms[3], ms[4], gm, wcat, bf, gq, gk, gv, wsp_s, bsp_s, wpb,
        tm=rows_s, mix_len=rows_s, mix_period=t_new, prompt=False)
    p_len = cache_k.shape[1]
    per_req = lambda z: z.reshape(nb_s, t_new, z.shape[-1])
    lfc = jnp.pad(cache_lf, ((0, 0), (0, 0), (0, LANES - N_HEADS)))
    a_s = _attn_sample(per_req(q_s), per_req(k_s), per_req(v_s), per_req(lf_s),
                       cache_k.reshape(nb_s, p_len, WIDTH_A), cache_v.reshape(nb_s, p_len, WIDTH_A), lfc)
    xs = _merge(a_s.reshape(1, rows_s, WIDTH_A), gas_s, mb_s, xs, ms[5], wpa, wo, rows_s)

    xp = _ffn(xp, mp[6], mp[7], mp[8], g2, wg2, wu2, wd2, tm)
    xs = _ffn(xs, ms[6], ms[7], ms[8], g2, wg2, wu2, wd2, rows_s)

    nb_p, s_p, _ = xp.shape
    heads = lambda z, nb, s: z.reshape(nb, s, N_HEADS, HEAD_DIM)
    outs = (heads(k_p, nb_p, s_p), heads(v_p, nb_p, s_p), lf_p[:, :, :N_HEADS],
            heads(per_req(k_s), nb_s, t_new), heads(per_req(v_s), nb_s, t_new),
            per_req(lf_s)[:, :, :N_HEADS], per_req(vbn_s))
    return xp, xs.reshape(nb_s, t_new, D_MODEL), outs


def kernel(x_prompt, x_sample, c_prompt, c_sample, cache_fox_k, cache_fox_v, cache_fox_logf, w_ada, b_ada, g_norm_ffn1, w_ffn1_gate, w_ffn1_up, w_ffn1_down, g_norm_mix, w_in, b_forget, g_q, g_k, g_gmlp_v, w_spatial, b_spatial, w_proj_a, w_proj_b, w_out, g_norm_ffn2, w_ffn2_gate, w_ffn2_up, w_ffn2_down):
    depth = w_ada.shape[0]
    nb_p, s_p, _ = x_prompt.shape
    nb_s = x_sample.shape[0]
    tm = min(512, s_p)
    t_attn = min(512, s_p // 2)
    xp, xs = x_prompt, x_sample
    c_all = jnp.concatenate([c_prompt, c_sample], axis=0)
    rows = c_all.shape[0]
    c_all = jnp.pad(c_all, ((0, -rows % 8), (0, 0)))
    stacks = [[] for _ in range(7)]
    for l in range(depth):
        mod = _ada(c_all, w_ada[l], b_ada[l])
        mp = [mod[:nb_p, i * D_MODEL:(i + 1) * D_MODEL] for i in range(N_MOD)]
        ms = [mod[nb_p:nb_p + nb_s, i * D_MODEL:(i + 1) * D_MODEL] for i in range(N_MOD)]
        xp, xs, outs = _layer(
            xp, xs, mp, ms, cache_fox_k[l], cache_fox_v[l], cache_fox_logf[l],
            g_norm_ffn1[l], w_ffn1_gate[l], w_ffn1_up[l], w_ffn1_down[l],
            g_norm_mix[l], w_in[l], b_forget[l], g_q[l], g_k[l], g_gmlp_v[l], w_spatial[l], b_spatial[l],
            w_proj_a[l], w_proj_b[l], w_out[l],
            g_norm_ffn2[l], w_ffn2_gate[l], w_ffn2_up[l], w_ffn2_down[l], tm=tm, t_attn=t_attn)
        for st, o in zip(stacks, outs):
            st.append(o)
    return (xp, xs) + tuple(jnp.stack(st) for st in stacks)
```

```python
import functools

import jax
import jax.numpy as jnp
from jax import lax
from jax.experimental import pallas as pl
from jax.experimental.pallas import tpu as pltpu

D_MODEL = 1024
N_HEADS = 8
HEAD_DIM = 64
WIDTH_A = N_HEADS * HEAD_DIM
N_GROUPS = 4
GROUP_DIM = 128
WIDTH_B = N_GROUPS * GROUP_DIM
GMLP_CHUNK = 128
D_FF = 2816
N_MOD = 9
EPS = 1e-6
ATTN_SCALE = HEAD_DIM ** -0.5
LOG2E = 1.4426950408889634

LANES = 128
SUBLANES = 8
N_PAIRS = N_HEADS // 2
VMEM_LIMIT = 56 * 1024 * 1024

F32 = jnp.float32
BF16 = jnp.bfloat16


def _const_spec(shape):
    nd = len(shape)
    return pl.BlockSpec(shape, lambda *_: (0,) * nd, pipeline_mode=pl.Buffered(1))


def _params(sem):
    return pltpu.CompilerParams(dimension_semantics=sem, vmem_limit_bytes=VMEM_LIMIT)


def _rms(x):
    return x * lax.rsqrt(jnp.mean(x * x, axis=-1, keepdims=True) + EPS)


def _split3(c):
    hi = c.astype(BF16).astype(F32)
    r = c - hi
    mid = r.astype(BF16).astype(F32)
    lo = (r - mid).astype(BF16).astype(F32)
    return hi, mid, lo


def _tril_ones(n, period=None):
    row = lax.broadcasted_iota(jnp.int32, (n, n), 0)
    col = lax.broadcasted_iota(jnp.int32, (n, n), 1)
    keep = col <= row
    if period is not None and period != n:
        keep = keep & ((row // period) == (col // period))
    return keep


def _cumsum_rows(x):
    n, width = x.shape
    groups = n // SUBLANES
    x3 = x.reshape(groups, SUBLANES, width)
    sub = lax.broadcasted_iota(jnp.int32, (1, SUBLANES, width), 1)
    k = 1
    while k < SUBLANES:
        x3 = x3 + jnp.where(sub >= k, pltpu.roll(x3, k, 1), 0.0)
        k *= 2
    tot = x3[:, SUBLANES - 1:SUBLANES, :]
    k = 1
    while k < groups:
        tot = tot + jnp.concatenate([jnp.zeros((k, 1, width), F32), tot[:groups - k]], axis=0)
        k *= 2
    before = jnp.concatenate([jnp.zeros((1, 1, width), F32), tot[:groups - 1]], axis=0)
    return (x3 + before).reshape(n, width)


def _head_norm(slab, gain):
    lane = lax.broadcasted_iota(jnp.int32, (1, LANES), 1)
    lo = lane < HEAD_DIM
    sq = slab * slab
    ss_lo = jnp.sum(jnp.where(lo, sq, 0.0), axis=-1, keepdims=True)
    ss_hi = jnp.sum(jnp.where(lo, 0.0, sq), axis=-1, keepdims=True)
    r = jnp.where(lo, lax.rsqrt(ss_lo / HEAD_DIM + EPS), lax.rsqrt(ss_hi / HEAD_DIM + EPS))
    return slab * r * gain


def _head_tiles(slab, fill_even, fill_odd):
    lane = lax.broadcasted_iota(jnp.int32, (1, LANES), 1)
    lo = lane < HEAD_DIM
    even = jnp.where(lo, slab, fill_even)
    odd = jnp.where(lo, pltpu.roll(slab, HEAD_DIM, 1), fill_odd)
    return even, odd


def _aug_q(pieces, h):
    lane = lax.broadcasted_iota(jnp.int32, (1, LANES), 1)
    hi, mid, lo = (p[:, h:h + 1] for p in pieces)
    ones = jnp.where((lane >= HEAD_DIM + 3) & (lane < HEAD_DIM + 6), 1.0, 0.0)
    return jnp.where(lane == HEAD_DIM, hi, jnp.where(lane == HEAD_DIM + 1, mid,
                     jnp.where(lane == HEAD_DIM + 2, lo, ones)))


def _aug_k(pieces, h):
    lane = lax.broadcasted_iota(jnp.int32, (1, LANES), 1)
    hi, mid, lo = (-p[:, h:h + 1] for p in pieces)
    ones = jnp.where((lane >= HEAD_DIM) & (lane < HEAD_DIM + 3), 1.0, 0.0)
    return jnp.where(lane == HEAD_DIM + 3, hi, jnp.where(lane == HEAD_DIM + 4, mid,
                     jnp.where(lane == HEAD_DIM + 5, lo, ones)))


def _log_sigmoid(x):
    return jnp.minimum(x, 0.0) - jnp.log1p(jnp.exp(-jnp.abs(x)))


def _ada_kernel(c_ref, w_ref, b_ref, o_ref):
    h = jax.nn.silu(c_ref[...]).astype(BF16)
    o_ref[...] = jnp.dot(h, w_ref[...].astype(BF16), preferred_element_type=F32) + b_ref[...]


def _ada(c, w_ada, b_ada):
    rows = c.shape[0]
    n = w_ada.shape[1]
    tn = D_MODEL
    return pl.pallas_call(
        _ada_kernel,
        grid=(n // tn,),
        in_specs=[pl.BlockSpec((rows, D_MODEL), lambda j: (0, 0)),
                  pl.BlockSpec((D_MODEL, tn), lambda j: (0, j)),
                  pl.BlockSpec((1, tn), lambda j: (0, j))],
        out_specs=pl.BlockSpec((rows, tn), lambda j: (0, j)),
        out_shape=jax.ShapeDtypeStruct((rows, n), F32),
        compiler_params=_params(("arbitrary",)),
        name="ada_modulation",
    )(c, w_ada, b_ada.reshape(1, n))


def _flat(x3):
    return x3.reshape(x3.shape[0] * x3.shape[1], x3.shape[2])


def _merge_rows(a_ref, gas_ref, mb_ref, wpa_ref, wo_ref):
    pa = jnp.dot(_flat(a_ref[...]), wpa_ref[...], preferred_element_type=F32)
    m = _flat(gas_ref[...]).astype(F32) * pa + _flat(mb_ref[...]).astype(F32)
    return jnp.dot(m.astype(BF16), wo_ref[...], preferred_element_type=F32)


def _ffn_kernel(*refs, merge):
    if merge:
        (x_ref, a_ref, gas_ref, mb_ref, mgate_ref, wpa_ref, wo_ref,
         shift_ref, scale_ref, gate_ref, g_ref, wg_ref, wu_ref, wd_ref, o_ref) = refs
        x3 = x_ref[...]
        x3 = x3 + mgate_ref[...] * _merge_rows(a_ref, gas_ref, mb_ref, wpa_ref, wo_ref).reshape(x3.shape)
    else:
        x_ref, shift_ref, scale_ref, gate_ref, g_ref, wg_ref, wu_ref, wd_ref, o_ref = refs
        x3 = x_ref[...]
    h = _flat((_rms(x3) * g_ref[...]) * (1.0 + scale_ref[...]) + shift_ref[...]).astype(BF16)
    a = jax.nn.silu(jnp.dot(h, wg_ref[...], preferred_element_type=F32))
    a = a * jnp.dot(h, wu_ref[...], preferred_element_type=F32)
    y = jnp.dot(a.astype(BF16), wd_ref[...], preferred_element_type=F32)
    o_ref[...] = x3 + 0.5 * gate_ref[...] * y.reshape(x3.shape)


def _row_specs(nbk, tm):
    row = lambda w: pl.BlockSpec((nbk, tm, w), lambda b, i: (b, i, 0))
    mod = pl.BlockSpec((nbk, 1, D_MODEL), lambda b, i: (b, 0, 0))
    return row, mod


def _ffn(x, shift, scale, gate, g, wg, wu, wd, tm, nbk, merge=None):
    nb, s, _ = x.shape
    row, mod = _row_specs(nbk, tm)
    weights = [_const_spec((1, 1, D_MODEL)), _const_spec(wg.shape), _const_spec(wu.shape), _const_spec(wd.shape)]
    args = [shift, scale, gate, g.reshape(1, 1, D_MODEL), wg, wu, wd]
    in_specs = [mod, mod, mod] + weights
    if merge is not None:
        a, gas, mb, mgate, wpa, wo = merge
        args = [a, gas, mb, mgate, wpa, wo] + args
        in_specs = [row(WIDTH_A), row(D_MODEL), row(D_MODEL), mod, _const_spec(wpa.shape),
                    _const_spec(wo.shape)] + in_specs
    return pl.pallas_call(
        functools.partial(_ffn_kernel, merge=merge is not None),
        grid=(nb // nbk, s // tm),
        in_specs=[row(D_MODEL)] + in_specs,
        out_specs=row(D_MODEL),
        out_shape=jax.ShapeDtypeStruct(x.shape, F32),
        compiler_params=_params(("arbitrary", "arbitrary")),
        name="ffn_half_step_merged" if merge is not None else "ffn_half_step",
    )(x, *args)


def _mix_project(x_ref, shift_ref, scale_ref, g_ref, wcat_ref, z_ref):
    n = _flat((_rms(x_ref[...]) * g_ref[...]) * (1.0 + scale_ref[...]) + shift_ref[...])
    z_ref[...] = jnp.dot(n.astype(BF16), wcat_ref[...], preferred_element_type=F32)


def _mix_finish(z_ref, first_tile, bf_ref, gq_ref, gk_ref, gv_ref, wsp_ref, bsp_ref, wpb_ref, outs,
                *, tm, mix_len, mix_period, prompt):
    if prompt:
        qa_ref, ka_ref, va_ref, ko_ref, vo_ref, lf_ref, gas_ref, mb_ref, carry_ref = outs
    else:
        qo_ref, ko_ref, vo_ref, lf_ref, vbn_ref, gas_ref, mb_ref = outs
    o_q, o_k, o_v = 0, WIDTH_A, 2 * WIDTH_A
    o_zb = 3 * WIDTH_A
    o_ga = o_zb + 2 * WIDTH_B
    o_gb = o_ga + D_MODEL
    o_f = o_gb + D_MODEL

    lane = lax.broadcasted_iota(jnp.int32, (1, LANES), 1)
    logf = jnp.where(lane < N_HEADS, _log_sigmoid(z_ref[:, o_f:o_f + LANES] + bf_ref[...]), 0.0)
    lf_ref[0] = logf

    if prompt:
        c = _cumsum_rows(logf) + jnp.where(first_tile, 0.0, carry_ref[...])
        carry_ref[...] = c[tm - 1:tm, :]
        pieces = _split3(c * LOG2E)

    for p in range(N_PAIRS):
        sl = slice(p * LANES, (p + 1) * LANES)
        qn = _head_norm(z_ref[:, o_q + p * LANES:o_q + (p + 1) * LANES], gq_ref[...])
        kn = _head_norm(z_ref[:, o_k + p * LANES:o_k + (p + 1) * LANES], gk_ref[...])
        v = z_ref[:, o_v + p * LANES:o_v + (p + 1) * LANES]
        ko_ref[0, :, sl] = kn
        vo_ref[0, :, sl] = v
        if prompt:
            he, ho = 2 * p, 2 * p + 1
            qe, qo = _head_tiles(qn * (ATTN_SCALE * LOG2E), _aug_q(pieces, he), _aug_q(pieces, ho))
            ke, ko = _head_tiles(kn, _aug_k(pieces, he), _aug_k(pieces, ho))
            ve, vo = _head_tiles(v, 1.0, 1.0)
            qa_ref[0, he] = qe.astype(BF16)
            qa_ref[0, ho] = qo.astype(BF16)
            ka_ref[0, he] = ke.astype(BF16)
            ka_ref[0, ho] = ko.astype(BF16)
            va_ref[0, he] = ve.astype(BF16)
            va_ref[0, ho] = vo.astype(BF16)
        else:
            qo_ref[0, :, sl] = qn

    zb = jax.nn.gelu(z_ref[:, o_zb:o_zb + 2 * WIDTH_B])
    u = zb[:, :WIDTH_B]
    vbn = _rms(zb[:, WIDTH_B:]) * gv_ref[...]
    if not prompt:
        vbn_ref[0] = vbn
    vbb = vbn.astype(BF16)
    keep = _tril_ones(mix_len, mix_period)
    wm = [jnp.where(keep, wsp_ref[g], 0.0).astype(BF16) for g in range(N_GROUPS)]
    rows = []
    for c0 in range(0, tm, mix_len):
        cols = [jnp.dot(wm[g], vbb[c0:c0 + mix_len, g * GROUP_DIM:(g + 1) * GROUP_DIM],
                        preferred_element_type=F32) for g in range(N_GROUPS)]
        rows.append(jnp.concatenate(cols, axis=1) + bsp_ref[...])
    mixed = rows[0] if len(rows) == 1 else jnp.concatenate(rows, axis=0)
    b = (u * mixed).astype(BF16)
    pb = jnp.dot(b, wpb_ref[...], preferred_element_type=F32)
    mb_ref[0] = (jax.nn.sigmoid(z_ref[:, o_gb:o_gb + D_MODEL]) * pb).astype(BF16)
    gas_ref[0] = jax.nn.sigmoid(z_ref[:, o_ga:o_ga + D_MODEL]).astype(BF16)


def _mix_kernel(x_ref, shift_ref, scale_ref, g_ref, wcat_ref, bf_ref, gq_ref, gk_ref, gv_ref,
                wsp_ref, bsp_ref, wpb_ref, *rest, n_tiles, **static):
    n_z = min(n_tiles, 2)
    z_refs = rest[-n_z:]
    outs = rest[:-n_z]
    project = functools.partial(_mix_project, x_ref, shift_ref, scale_ref, g_ref, wcat_ref)
    finish = functools.partial(_mix_finish, bf_ref=bf_ref, gq_ref=gq_ref, gk_ref=gk_ref, gv_ref=gv_ref,
                               wsp_ref=wsp_ref, bsp_ref=bsp_ref, wpb_ref=wpb_ref, outs=outs, **static)
    i = pl.program_id(1)
    if n_tiles == 1:
        project(z_refs[0])
        finish(z_refs[0], True)
        return

    @pl.when(i == 0)
    def _():
        project(z_refs[0])

    for parity in range(2):
        @pl.when((i > 0) & (i < n_tiles) & (i % 2 == parity))
        def _():
            project(z_refs[parity])
            finish(z_refs[1 - parity], i == 1)

    @pl.when(i == n_tiles)
    def _():
        finish(z_refs[(n_tiles - 1) % 2], n_tiles == 1)


def _mix(x, shift, scale, g, wcat, bf, gq, gk, gv, wsp, bsp, wpb, *, rows, nbk, mix_len, mix_period, prompt):
    nb, s, _ = x.shape
    tm = nbk * rows
    n_tiles = s // rows
    steps = n_tiles if n_tiles == 1 else n_tiles + 1
    done = lambda i: jnp.maximum(i - 1, 0) if n_tiles > 1 else i
    row = lambda w: pl.BlockSpec((1, tm, w), lambda b, i: (b, done(i), 0))
    head = pl.BlockSpec((1, N_HEADS, tm, LANES), lambda b, i: (b, 0, done(i), 0))
    x_spec = pl.BlockSpec((nbk, rows, D_MODEL), lambda b, i: (b, jnp.minimum(i, n_tiles - 1), 0))
    mod = pl.BlockSpec((nbk, 1, D_MODEL), lambda b, i: (b, 0, 0))
    sds = jax.ShapeDtypeStruct
    tok = lambda w, dt: sds((nb // nbk, nbk * s, w), dt)
    hm = sds((nb // nbk, N_HEADS, nbk * s, LANES), BF16)
    if prompt:
        out_specs = [head, head, head, row(WIDTH_A), row(WIDTH_A), row(LANES), row(D_MODEL), row(D_MODEL)]
        out_shape = [hm, hm, hm, tok(WIDTH_A, F32), tok(WIDTH_A, F32), tok(LANES, F32),
                     tok(D_MODEL, BF16), tok(D_MODEL, BF16)]
        scratch = [pltpu.VMEM((1, LANES), F32)]
    else:
        out_specs = [row(WIDTH_A), row(WIDTH_A), row(WIDTH_A), row(LANES), row(WIDTH_B),
                     row(D_MODEL), row(D_MODEL)]
        out_shape = [tok(WIDTH_A, F32), tok(WIDTH_A, F32), tok(WIDTH_A, F32), tok(LANES, F32),
                     tok(WIDTH_B, F32), tok(D_MODEL, BF16), tok(D_MODEL, BF16)]
        scratch = []
    scratch += [pltpu.VMEM((tm, wcat.shape[1]), F32)] * min(n_tiles, 2)
    kern = functools.partial(_mix_kernel, n_tiles=n_tiles, tm=tm, mix_len=mix_len, mix_period=mix_period,
                             prompt=prompt)
    return pl.pallas_call(
        kern,
        grid=(nb // nbk, steps),
        in_specs=[x_spec, mod, mod, _const_spec((1, 1, D_MODEL)),
                  _const_spec(wcat.shape), _const_spec(bf.shape), _const_spec(gq.shape),
                  _const_spec(gk.shape), _const_spec(gv.shape), _const_spec(wsp.shape),
                  _const_spec(bsp.shape), _const_spec(wpb.shape)],
        out_specs=out_specs,
        out_shape=out_shape,
        scratch_shapes=scratch,
        compiler_params=_params(("arbitrary", "arbitrary")),
        name="mixer_in_prompt" if prompt else "mixer_in_sample",
    )(x, shift, scale, g.reshape(1, 1, D_MODEL), wcat, bf, gq, gk, gv, wsp, bsp, wpb)


def _scores(q, k):
    return lax.dot_general(q, k, (((1,), (1,)), ((), ())), preferred_element_type=F32)


def _softmax_update(s, v, m_old, acc_old, mask):
    if mask is not None:
        s = jnp.where(mask, s, -jnp.inf)
    m_new = jnp.maximum(m_old, jnp.max(s, axis=-1, keepdims=True))
    cols = s.shape[1]
    m_wide = jnp.concatenate([m_new] * (cols // LANES), axis=1) if cols >= LANES else m_new[:, :cols]
    p = jnp.exp2(s - m_wide)
    alpha = jnp.exp2(m_old - m_new)
    acc = alpha * acc_old + jnp.dot(p.astype(BF16), v, preferred_element_type=F32)
    return m_new, acc


def _pair_output(acc_even, acc_odd):
    lane = lax.broadcasted_iota(jnp.int32, (1, LANES), 1)
    oe = acc_even * pltpu.roll(1.0 / acc_even, HEAD_DIM, 1)
    oo = acc_odd * pltpu.roll(1.0 / acc_odd, HEAD_DIM, 1)
    return jnp.where(lane < HEAD_DIM, oe, pltpu.roll(oo, HEAD_DIM, 1))


def _attn_kernel(q_ref, qn_ref, k_ref, v_ref, o_ref, s0_ref, s1_ref, acc_ref, m_ref, *, tq, tk):
    qi = pl.program_id(2)
    m_ref[...] = jnp.full(m_ref.shape, -jnp.inf, F32)
    acc_ref[...] = jnp.zeros(acc_ref.shape, F32)

    def keys(j):
        return pl.ds(pl.multiple_of(j * tk, tk), tk)

    def qk(src_ref, j, s_ref, r0=0):
        for h in range(2):
            s_ref[h, r0:, :] = _scores(src_ref[0, h, r0:, :], k_ref[0, h, keys(j), :])

    def soft_pv(j, s_ref, mask, r0=0):
        for h in range(2):
            m_new, acc = _softmax_update(s_ref[h, r0:, :], v_ref[0, h, keys(j), :],
                                         m_ref[h, r0:, :], acc_ref[h, r0:, :], mask)
            m_ref[h, r0:, :] = m_new
            acc_ref[h, r0:, :] = acc

    @pl.when(qi == 0)
    def _():
        qk(q_ref, 0, s0_ref)

    def body(u, carry):
        j = 2 * u
        qk(q_ref, j + 1, s1_ref)
        soft_pv(j, s0_ref, None)
        qk(q_ref, j + 2, s0_ref)
        soft_pv(j + 1, s1_ref, None)
        return carry

    lax.fori_loop(0, qi, body, 0)

    row = lax.broadcasted_iota(jnp.int32, (tq, tk), 0)
    col = lax.broadcasted_iota(jnp.int32, (tq, tk), 1)
    qk(q_ref, 2 * qi + 1, s1_ref, tk)
    soft_pv(2 * qi, s0_ref, col <= row)
    qk(qn_ref, 0, s0_ref)
    soft_pv(2 * qi + 1, s1_ref, _tril_ones(tk), tk)

    o_ref[0] = _pair_output(acc_ref[0], acc_ref[1]).astype(o_ref.dtype)


def _attn(qa, ka, va, tk):
    nb, _, s, _ = qa.shape
    tq = 2 * tk
    nq = s // tq
    kv = pl.BlockSpec((1, 2, s, LANES), lambda b, p, i: (b, p, 0, 0))
    return pl.pallas_call(
        functools.partial(_attn_kernel, tq=tq, tk=tk),
        grid=(nb, N_PAIRS, nq),
        in_specs=[pl.BlockSpec((1, 2, tq, LANES), lambda b, p, i: (b, p, i, 0)),
                  pl.BlockSpec((1, 2, tq, LANES), lambda b, p, i: (b, p, jnp.minimum(i + 1, nq - 1), 0)),
                  kv, kv],
        out_specs=pl.BlockSpec((1, tq, LANES), lambda b, p, i: (b, i, p)),
        out_shape=jax.ShapeDtypeStruct((nb, s, WIDTH_A), BF16),
        scratch_shapes=[pltpu.VMEM((2, tq, tk), F32), pltpu.VMEM((2, tq, tk), F32),
                        pltpu.VMEM((2, tq, LANES), F32), pltpu.VMEM((2, tq, LANES), F32)],
        compiler_params=_params(("arbitrary", "arbitrary", "arbitrary")),
        name="fox_prompt_attention",
    )(qa, qa, ka, va)


def _attn_sample_kernel(q_ref, k_ref, v_ref, lf_ref, kc_ref, vc_ref, lfc_ref, o_ref, *, t_new):
    c_cache = _cumsum_rows(lfc_ref[0])
    p_len = c_cache.shape[0]
    c_new = _cumsum_rows(lf_ref[0][:, :N_HEADS]) + c_cache[p_len - 1:p_len, :]
    pc_cache = _split3(c_cache * LOG2E)
    pc_new = _split3(c_new * LOG2E)
    causal = _tril_ones(t_new)
    lane = lax.broadcasted_iota(jnp.int32, (1, LANES), 1)
    lo = lane < HEAD_DIM
    widen = lambda x: jnp.pad(x, ((0, 0), (0, LANES - HEAD_DIM)))
    for p in range(N_PAIRS):
        sl = slice(p * LANES, (p + 1) * LANES)
        he, ho = 2 * p, 2 * p + 1
        q_t = _head_tiles(q_ref[0, :, sl] * (ATTN_SCALE * LOG2E), _aug_q(pc_new, he), _aug_q(pc_new, ho))
        kn_t = _head_tiles(k_ref[0, :, sl], _aug_k(pc_new, he), _aug_k(pc_new, ho))
        vn_t = _head_tiles(v_ref[0, :, sl], 1.0, 1.0)
        accs = []
        for j, h in enumerate((he, ho)):
            kc_t = jnp.where(lo, widen(kc_ref[0, :, h, :]), _aug_k(pc_cache, h)).astype(BF16)
            vc_t = jnp.where(lo, widen(vc_ref[0, :, h, :]), 1.0).astype(BF16)
            q = q_t[j].astype(BF16)
            m0 = jnp.full((t_new, LANES), -jnp.inf, F32)
            a0 = jnp.zeros((t_new, LANES), F32)
            m1, a1 = _softmax_update(_scores(q, kc_t), vc_t, m0, a0, None)
            _, a2 = _softmax_update(_scores(q, kn_t[j].astype(BF16)), vn_t[j].astype(BF16), m1, a1, causal)
            accs.append(a2)
        o_ref[0, :, sl] = _pair_output(accs[0], accs[1]).astype(o_ref.dtype)


def _attn_sample(q, k, v, lf, kc, vc, lfc):
    nb, t_new, _ = q.shape
    p_len = kc.shape[1]
    new = lambda w: pl.BlockSpec((1, t_new, w), lambda b: (b, 0, 0))
    cache = pl.BlockSpec((1, p_len, N_HEADS, HEAD_DIM), lambda b: (b, 0, 0, 0))
    return pl.pallas_call(
        functools.partial(_attn_sample_kernel, t_new=t_new),
        grid=(nb,),
        in_specs=[new(WIDTH_A), new(WIDTH_A), new(WIDTH_A), new(LANES), cache, cache,
                  pl.BlockSpec((1, p_len, N_HEADS), lambda b: (b, 0, 0))],
        out_specs=new(WIDTH_A),
        out_shape=jax.ShapeDtypeStruct((nb, t_new, WIDTH_A), BF16),
        compiler_params=_params(("arbitrary",)),
        name="fox_sample_attention",
    )(q, k, v, lf, kc, vc, lfc)


def _pad_lanes(v, fill=0.0):
    v = v.reshape(1, -1)
    return jnp.pad(v, ((0, 0), (0, LANES - v.shape[1])), constant_values=fill)


def _layer(xp, xs, mp, ms, cache_k, cache_v, cache_lf, g1, wg1, wu1, wd1, gm, w_in, b_forget, g_q, g_k, g_v,
           w_spatial, b_spatial, w_proj_a, w_proj_b, w_out, g2, wg2, wu2, wd2, *, tm, t_attn):
    nb_p, s_p, _ = xp.shape
    nb_s, t_new, _ = xs.shape
    rows_s = nb_s * t_new
    mp = [m[:, None, :] for m in mp]
    ms = [m[:, None, :] for m in ms]

    wg1, wu1, wd1 = wg1.astype(BF16), wu1.astype(BF16), wd1.astype(BF16)
    wg2, wu2, wd2 = wg2.astype(BF16), wu2.astype(BF16), wd2.astype(BF16)
    o_f = 3 * WIDTH_A
    o_zb = o_f + N_HEADS
    wf = jnp.pad(w_in[:, o_f:o_zb], ((0, 0), (0, LANES - N_HEADS)))
    wcat = jnp.concatenate([w_in[:, :o_f], w_in[:, o_zb:], wf], axis=1).astype(BF16)
    bf = _pad_lanes(b_forget)
    gq = jnp.tile(g_q, 2).reshape(1, LANES)
    gk = jnp.tile(g_k, 2).reshape(1, LANES)
    gv = g_v.reshape(1, WIDTH_B)
    bsp = jnp.repeat(b_spatial.T, GROUP_DIM, axis=1)
    wpa = w_proj_a.astype(BF16)
    wpb = w_proj_b.astype(BF16)
    wo = w_out.astype(BF16)

    xp = _ffn(xp, mp[0], mp[1], mp[2], g1, wg1, wu1, wd1, tm, 1)
    xs = _ffn(xs, ms[0], ms[1], ms[2], g1, wg1, wu1, wd1, t_new, nb_s)

    qa, ka, va, k_p, v_p, lf_p, gas, mb = _mix(
        xp, mp[3], mp[4], gm, wcat, bf, gq, gk, gv, w_spatial, bsp, wpb,
        rows=tm, nbk=1, mix_len=GMLP_CHUNK, mix_period=GMLP_CHUNK, prompt=True)
    a = _attn(qa, ka, va, t_attn)

    wsp_s = jnp.tile(w_spatial[:, :t_new, :t_new], (1, nb_s, nb_s))
    bsp_s = jnp.tile(bsp[:t_new], (nb_s, 1))
    q_s, k_s, v_s, lf_s, vbn_s, gas_s, mb_s = _mix(
        xs, ms[3], ms[4], gm, wcat, bf, gq, gk, gv, wsp_s, bsp_s, wpb,
        rows=t_new, nbk=nb_s, mix_len=rows_s, mix_period=t_new, prompt=False)
    per_req = lambda z: z.reshape(nb_s, t_new, z.shape[-1])
    a_s = _attn_sample(per_req(q_s), per_req(k_s), per_req(v_s), per_req(lf_s), cache_k, cache_v, cache_lf)

    xp = _ffn(xp, mp[6], mp[7], mp[8], g2, wg2, wu2, wd2, tm, 1, merge=(a, gas, mb, mp[5], wpa, wo))
    xs = _ffn(xs, ms[6], ms[7], ms[8], g2, wg2, wu2, wd2, t_new, nb_s,
              merge=(a_s, per_req(gas_s), per_req(mb_s), ms[5], wpa, wo))

    heads = lambda z, nb, s: z.reshape(nb, s, N_HEADS, HEAD_DIM)
    outs = (heads(k_p, nb_p, s_p), heads(v_p, nb_p, s_p), lf_p[:, :, :N_HEADS],
            heads(per_req(k_s), nb_s, t_new), heads(per_req(v_s), nb_s, t_new),
            per_req(lf_s)[:, :, :N_HEADS], per_req(vbn_s))
    return xp, xs, outs


def kernel(x_prompt, x_sample, c_prompt, c_sample, cache_fox_k, cache_fox_v, cache_fox_logf, w_ada, b_ada, g_norm_ffn1, w_ffn1_gate, w_ffn1_up, w_ffn1_down, g_norm_mix, w_in, b_forget, g_q, g_k, g_gmlp_v, w_spatial, b_spatial, w_proj_a, w_proj_b, w_out, g_norm_ffn2, w_ffn2_gate, w_ffn2_up, w_ffn2_down):
    depth = w_ada.shape[0]
    nb_p, s_p, _ = x_prompt.shape
    nb_s = x_sample.shape[0]
    tm = min(512, s_p)
    t_attn = min(512, s_p // 2)
    xp, xs = x_prompt, x_sample
    c_all = jnp.concatenate([c_prompt, c_sample], axis=0)
    rows = c_all.shape[0]
    c_all = jnp.pad(c_all, ((0, -rows % 8), (0, 0)))
    stacks = [[] for _ in range(7)]
    for l in range(depth):
        mod = _ada(c_all, w_ada[l], b_ada[l])
        mp = [mod[:nb_p, i * D_MODEL:(i + 1) * D_MODEL] for i in range(N_MOD)]
        ms = [mod[nb_p:nb_p + nb_s, i * D_MODEL:(i + 1) * D_MODEL] for i in range(N_MOD)]
        xp, xs, outs = _layer(
            xp, xs, mp, ms, cache_fox_k[l], cache_fox_v[l], cache_fox_logf[l],
            g_norm_ffn1[l], w_ffn1_gate[l], w_ffn1_up[l], w_ffn1_down[l],
            g_norm_mix[l], w_in[l], b_forget[l], g_q[l], g_k[l], g_gmlp_v[l], w_spatial[l], b_spatial[l],
            w_proj_a[l], w_proj_b[l], w_out[l],
            g_norm_ffn2[l], w_ffn2_gate[l], w_ffn2_up[l], w_ffn2_down[l], tm=tm, t_attn=t_attn)
        for st, o in zip(stacks, outs):
            st.append(o)
    return (xp, xs) + tuple(jnp.stack(st) for st in stacks)
```

```python
import functools

import jax
import jax.numpy as jnp
from jax import lax
from jax.experimental import pallas as pl
from jax.experimental.pallas import tpu as pltpu

D_MODEL = 1024
N_HEADS = 8
HEAD_DIM = 64
WIDTH_A = N_HEADS * HEAD_DIM
N_GROUPS = 4
GROUP_DIM = 128
WIDTH_B = N_GROUPS * GROUP_DIM
GMLP_CHUNK = 128
D_FF = 2816
N_MOD = 9
EPS = 1e-6
ATTN_SCALE = HEAD_DIM ** -0.5
LOG2E = 1.4426950408889634

LANES = 128
SUBLANES = 8
N_PAIRS = N_HEADS // 2
VMEM_LIMIT = 56 * 1024 * 1024

F32 = jnp.float32
BF16 = jnp.bfloat16


def _const_spec(shape):
    nd = len(shape)
    return pl.BlockSpec(shape, lambda *_: (0,) * nd, pipeline_mode=pl.Buffered(1))


def _params(sem):
    return pltpu.CompilerParams(dimension_semantics=sem, vmem_limit_bytes=VMEM_LIMIT)


def _rms(x):
    return x * lax.rsqrt(jnp.mean(x * x, axis=-1, keepdims=True) + EPS)


def _split3(c):
    hi = c.astype(BF16).astype(F32)
    r = c - hi
    mid = r.astype(BF16).astype(F32)
    lo = (r - mid).astype(BF16).astype(F32)
    return hi, mid, lo


def _tril_ones(n, period=None):
    row = lax.broadcasted_iota(jnp.int32, (n, n), 0)
    col = lax.broadcasted_iota(jnp.int32, (n, n), 1)
    keep = col <= row
    if period is not None and period != n:
        keep = keep & ((row // period) == (col // period))
    return keep


def _cumsum_rows(x):
    n, width = x.shape
    groups = n // SUBLANES
    x3 = x.reshape(groups, SUBLANES, width)
    sub = lax.broadcasted_iota(jnp.int32, (1, SUBLANES, width), 1)
    k = 1
    while k < SUBLANES:
        x3 = x3 + jnp.where(sub >= k, pltpu.roll(x3, k, 1), 0.0)
        k *= 2
    tot = x3[:, SUBLANES - 1:SUBLANES, :]
    k = 1
    while k < groups:
        tot = tot + jnp.concatenate([jnp.zeros((k, 1, width), F32), tot[:groups - k]], axis=0)
        k *= 2
    before = jnp.concatenate([jnp.zeros((1, 1, width), F32), tot[:groups - 1]], axis=0)
    return (x3 + before).reshape(n, width)


def _head_norm(slab, gain):
    lane = lax.broadcasted_iota(jnp.int32, (1, LANES), 1)
    lo = lane < HEAD_DIM
    sq = slab * slab
    ss_lo = jnp.sum(jnp.where(lo, sq, 0.0), axis=-1, keepdims=True)
    ss_hi = jnp.sum(jnp.where(lo, 0.0, sq), axis=-1, keepdims=True)
    r = jnp.where(lo, lax.rsqrt(ss_lo / HEAD_DIM + EPS), lax.rsqrt(ss_hi / HEAD_DIM + EPS))
    return slab * r * gain


def _head_tiles(slab, fill_even, fill_odd):
    lane = lax.broadcasted_iota(jnp.int32, (1, LANES), 1)
    lo = lane < HEAD_DIM
    even = jnp.where(lo, slab, fill_even)
    odd = jnp.where(lo, pltpu.roll(slab, HEAD_DIM, 1), fill_odd)
    return even, odd


def _aug_q(pieces, h):
    lane = lax.broadcasted_iota(jnp.int32, (1, LANES), 1)
    hi, mid, lo = (p[:, h:h + 1] for p in pieces)
    ones = jnp.where((lane >= HEAD_DIM + 3) & (lane < HEAD_DIM + 6), 1.0, 0.0)
    return jnp.where(lane == HEAD_DIM, hi, jnp.where(lane == HEAD_DIM + 1, mid,
                     jnp.where(lane == HEAD_DIM + 2, lo, ones)))


def _aug_k(pieces, h):
    lane = lax.broadcasted_iota(jnp.int32, (1, LANES), 1)
    hi, mid, lo = (-p[:, h:h + 1] for p in pieces)
    ones = jnp.where((lane >= HEAD_DIM) & (lane < HEAD_DIM + 3), 1.0, 0.0)
    return jnp.where(lane == HEAD_DIM + 3, hi, jnp.where(lane == HEAD_DIM + 4, mid,
                     jnp.where(lane == HEAD_DIM + 5, lo, ones)))


def _log_sigmoid(x):
    return jnp.minimum(x, 0.0) - jnp.log1p(jnp.exp(-jnp.abs(x)))


def _ada_kernel(c_ref, w_ref, b_ref, o_ref):
    h = jax.nn.silu(c_ref[...]).astype(BF16)
    o_ref[...] = jnp.dot(h, w_ref[...].astype(BF16), preferred_element_type=F32) + b_ref[...]


def _ada(c, w_ada, b_ada):
    rows = c.shape[0]
    n = w_ada.shape[1]
    tn = D_MODEL
    return pl.pallas_call(
        _ada_kernel,
        grid=(n // tn,),
        in_specs=[pl.BlockSpec((rows, D_MODEL), lambda j: (0, 0)),
                  pl.BlockSpec((D_MODEL, tn), lambda j: (0, j)),
                  pl.BlockSpec((1, tn), lambda j: (0, j))],
        out_specs=pl.BlockSpec((rows, tn), lambda j: (0, j)),
        out_shape=jax.ShapeDtypeStruct((rows, n), F32),
        compiler_params=_params(("arbitrary",)),
        name="ada_modulation",
    )(c, w_ada, b_ada.reshape(1, n))


def _flat(x3):
    return x3.reshape(x3.shape[0] * x3.shape[1], x3.shape[2])


def _merge_rows(a_ref, gas_ref, mb_ref, wpa_ref, wo_ref):
    pa = jnp.dot(_flat(a_ref[...]), wpa_ref[...], preferred_element_type=F32)
    m = _flat(gas_ref[...]).astype(F32) * pa + _flat(mb_ref[...]).astype(F32)
    return jnp.dot(m.astype(BF16), wo_ref[...], preferred_element_type=F32)


def _ffn_kernel(*refs, merge):
    if merge:
        (x_ref, a_ref, gas_ref, mb_ref, mgate_ref, wpa_ref, wo_ref,
         shift_ref, scale_ref, gate_ref, g_ref, wg_ref, wu_ref, wd_ref, o_ref) = refs
        x3 = x_ref[...]
        x3 = x3 + mgate_ref[...] * _merge_rows(a_ref, gas_ref, mb_ref, wpa_ref, wo_ref).reshape(x3.shape)
    else:
        x_ref, shift_ref, scale_ref, gate_ref, g_ref, wg_ref, wu_ref, wd_ref, o_ref = refs
        x3 = x_ref[...]
    h = _flat((_rms(x3) * g_ref[...]) * (1.0 + scale_ref[...]) + shift_ref[...]).astype(BF16)
    a = jax.nn.silu(jnp.dot(h, wg_ref[...], preferred_element_type=F32))
    a = a * jnp.dot(h, wu_ref[...], preferred_element_type=F32)
    y = jnp.dot(a.astype(BF16), wd_ref[...], preferred_element_type=F32)
    o_ref[...] = x3 + 0.5 * gate_ref[...] * y.reshape(x3.shape)


def _row_specs(nbk, tm):
    row = lambda w: pl.BlockSpec((nbk, tm, w), lambda b, i: (b, i, 0))
    mod = pl.BlockSpec((nbk, 1, D_MODEL), lambda b, i: (b, 0, 0))
    return row, mod


def _ffn(x, shift, scale, gate, g, wg, wu, wd, tm, nbk, merge=None):
    nb, s, _ = x.shape
    row, mod = _row_specs(nbk, tm)
    weights = [_const_spec((1, 1, D_MODEL)), _const_spec(wg.shape), _const_spec(wu.shape), _const_spec(wd.shape)]
    args = [shift, scale, gate, g.reshape(1, 1, D_MODEL), wg, wu, wd]
    in_specs = [mod, mod, mod] + weights
    if merge is not None:
        a, gas, mb, mgate, wpa, wo = merge
        args = [a, gas, mb, mgate, wpa, wo] + args
        in_specs = [row(WIDTH_A), row(D_MODEL), row(D_MODEL), mod, _const_spec(wpa.shape),
                    _const_spec(wo.shape)] + in_specs
    return pl.pallas_call(
        functools.partial(_ffn_kernel, merge=merge is not None),
        grid=(nb // nbk, s // tm),
        in_specs=[row(D_MODEL)] + in_specs,
        out_specs=row(D_MODEL),
        out_shape=jax.ShapeDtypeStruct(x.shape, F32),
        compiler_params=_params(("arbitrary", "arbitrary")),
        name="ffn_half_step_merged" if merge is not None else "ffn_half_step",
    )(x, *args)


def _mix_project(x_ref, shift_ref, scale_ref, g_ref, wcat_ref, z_ref):
    n = _flat((_rms(x_ref[...]) * g_ref[...]) * (1.0 + scale_ref[...]) + shift_ref[...])
    z_ref[...] = jnp.dot(n.astype(BF16), wcat_ref[...], preferred_element_type=F32)


def _mix_finish(z_ref, first_tile, bf_ref, gq_ref, gk_ref, gv_ref, wsp_ref, bsp_ref, wpb_ref, outs,
                *, tm, mix_len, mix_period, prompt):
    if prompt:
        qa_ref, ka_ref, va_ref, ko_ref, vo_ref, lf_ref, gas_ref, mb_ref, carry_ref = outs
    else:
        qo_ref, ko_ref, vo_ref, lf_ref, vbn_ref, gas_ref, mb_ref = outs
    o_q, o_k, o_v = 0, WIDTH_A, 2 * WIDTH_A
    o_zb = 3 * WIDTH_A
    o_ga = o_zb + 2 * WIDTH_B
    o_gb = o_ga + D_MODEL
    o_f = o_gb + D_MODEL

    lane = lax.broadcasted_iota(jnp.int32, (1, LANES), 1)
    logf = jnp.where(lane < N_HEADS, _log_sigmoid(z_ref[:, o_f:o_f + LANES] + bf_ref[...]), 0.0)
    if prompt:
        lf_ref[0] = logf.T[:N_HEADS, :]
    else:
        lf_ref[0] = logf

    if prompt:
        c = _cumsum_rows(logf) + jnp.where(first_tile, 0.0, carry_ref[...])
        carry_ref[...] = c[tm - 1:tm, :]
        pieces = _split3(c * LOG2E)

    for p in range(N_PAIRS):
        sl = slice(p * LANES, (p + 1) * LANES)
        qn = _head_norm(z_ref[:, o_q + p * LANES:o_q + (p + 1) * LANES], gq_ref[...])
        kn = _head_norm(z_ref[:, o_k + p * LANES:o_k + (p + 1) * LANES], gk_ref[...])
        v = z_ref[:, o_v + p * LANES:o_v + (p + 1) * LANES]
        ko_ref[0, :, sl] = kn
        vo_ref[0, :, sl] = v
        if prompt:
            he, ho = 2 * p, 2 * p + 1
            qe, qo = _head_tiles(qn * (ATTN_SCALE * LOG2E), _aug_q(pieces, he), _aug_q(pieces, ho))
            ke, ko = _head_tiles(kn, _aug_k(pieces, he), _aug_k(pieces, ho))
            ve, vo = _head_tiles(v, 1.0, 1.0)
            qa_ref[0, he] = qe.astype(BF16)
            qa_ref[0, ho] = qo.astype(BF16)
            ka_ref[0, he] = ke.astype(BF16)
            ka_ref[0, ho] = ko.astype(BF16)
            va_ref[0, he] = ve.astype(BF16)
            va_ref[0, ho] = vo.astype(BF16)
        else:
            qo_ref[0, :, sl] = qn

    zb = jax.nn.gelu(z_ref[:, o_zb:o_zb + 2 * WIDTH_B])
    u = zb[:, :WIDTH_B]
    vbn = _rms(zb[:, WIDTH_B:]) * gv_ref[...]
    if not prompt:
        vbn_ref[0] = vbn
    vbb = vbn.astype(BF16)
    keep = _tril_ones(mix_len, mix_period)
    wm = [jnp.where(keep, wsp_ref[g], 0.0).astype(BF16) for g in range(N_GROUPS)]
    rows = []
    for c0 in range(0, tm, mix_len):
        cols = [jnp.dot(wm[g], vbb[c0:c0 + mix_len, g * GROUP_DIM:(g + 1) * GROUP_DIM],
                        preferred_element_type=F32) for g in range(N_GROUPS)]
        rows.append(jnp.concatenate(cols, axis=1) + bsp_ref[...])
    mixed = rows[0] if len(rows) == 1 else jnp.concatenate(rows, axis=0)
    b = (u * mixed).astype(BF16)
    pb = jnp.dot(b, wpb_ref[...], preferred_element_type=F32)
    mb_ref[0] = (jax.nn.sigmoid(z_ref[:, o_gb:o_gb + D_MODEL]) * pb).astype(BF16)
    gas_ref[0] = jax.nn.sigmoid(z_ref[:, o_ga:o_ga + D_MODEL]).astype(BF16)


def _mix_kernel(x_ref, shift_ref, scale_ref, g_ref, wcat_ref, bf_ref, gq_ref, gk_ref, gv_ref,
                wsp_ref, bsp_ref, wpb_ref, *rest, n_tiles, **static):
    n_z = min(n_tiles, 2)
    z_refs = rest[-n_z:]
    outs = rest[:-n_z]
    project = functools.partial(_mix_project, x_ref, shift_ref, scale_ref, g_ref, wcat_ref)
    finish = functools.partial(_mix_finish, bf_ref=bf_ref, gq_ref=gq_ref, gk_ref=gk_ref, gv_ref=gv_ref,
                               wsp_ref=wsp_ref, bsp_ref=bsp_ref, wpb_ref=wpb_ref, outs=outs, **static)
    i = pl.program_id(1)
    if n_tiles == 1:
        project(z_refs[0])
        finish(z_refs[0], True)
        return

    @pl.when(i == 0)
    def _():
        project(z_refs[0])

    for parity in range(2):
        @pl.when((i > 0) & (i < n_tiles) & (i % 2 == parity))
        def _():
            project(z_refs[parity])
            finish(z_refs[1 - parity], i == 1)

    @pl.when(i == n_tiles)
    def _():
        finish(z_refs[(n_tiles - 1) % 2], n_tiles == 1)


def _mix(x, shift, scale, g, wcat, bf, gq, gk, gv, wsp, bsp, wpb, *, rows, nbk, mix_len, mix_period, prompt):
    nb, s, _ = x.shape
    tm = nbk * rows
    n_tiles = s // rows
    steps = n_tiles if n_tiles == 1 else n_tiles + 1
    done = lambda i: jnp.maximum(i - 1, 0) if n_tiles > 1 else i
    row = lambda w: pl.BlockSpec((1, tm, w), lambda b, i: (b, done(i), 0))
    head = pl.BlockSpec((1, N_HEADS, tm, LANES), lambda b, i: (b, 0, done(i), 0))
    x_spec = pl.BlockSpec((nbk, rows, D_MODEL), lambda b, i: (b, jnp.minimum(i, n_tiles - 1), 0))
    mod = pl.BlockSpec((nbk, 1, D_MODEL), lambda b, i: (b, 0, 0))
    sds = jax.ShapeDtypeStruct
    tok = lambda w, dt: sds((nb // nbk, nbk * s, w), dt)
    hm = sds((nb // nbk, N_HEADS, nbk * s, LANES), BF16)
    if prompt:
        lf_spec = pl.BlockSpec((1, N_HEADS, tm), lambda b, i: (b, 0, done(i)))
        out_specs = [head, head, head, row(WIDTH_A), row(WIDTH_A), lf_spec, row(D_MODEL), row(D_MODEL)]
        out_shape = [hm, hm, hm, tok(WIDTH_A, F32), tok(WIDTH_A, F32), sds((nb, N_HEADS, s), F32),
                     tok(D_MODEL, BF16), tok(D_MODEL, BF16)]
        scratch = [pltpu.VMEM((1, LANES), F32)]
    else:
        out_specs = [row(WIDTH_A), row(WIDTH_A), row(WIDTH_A), row(LANES), row(WIDTH_B),
                     row(D_MODEL), row(D_MODEL)]
        out_shape = [tok(WIDTH_A, F32), tok(WIDTH_A, F32), tok(WIDTH_A, F32), tok(LANES, F32),
                     tok(WIDTH_B, F32), tok(D_MODEL, BF16), tok(D_MODEL, BF16)]
        scratch = []
    scratch += [pltpu.VMEM((tm, wcat.shape[1]), F32)] * min(n_tiles, 2)
    kern = functools.partial(_mix_kernel, n_tiles=n_tiles, tm=tm, mix_len=mix_len, mix_period=mix_period,
                             prompt=prompt)
    return pl.pallas_call(
        kern,
        grid=(nb // nbk, steps),
        in_specs=[x_spec, mod, mod, _const_spec((1, 1, D_MODEL)),
                  _const_spec(wcat.shape), _const_spec(bf.shape), _const_spec(gq.shape),
                  _const_spec(gk.shape), _const_spec(gv.shape), _const_spec(wsp.shape),
                  _const_spec(bsp.shape), _const_spec(wpb.shape)],
        out_specs=out_specs,
        out_shape=out_shape,
        scratch_shapes=scratch,
        compiler_params=_params(("arbitrary", "arbitrary")),
        name="mixer_in_prompt" if prompt else "mixer_in_sample",
    )(x, shift, scale, g.reshape(1, 1, D_MODEL), wcat, bf, gq, gk, gv, wsp, bsp, wpb)


def _scores(q, k):
    return lax.dot_general(q, k, (((1,), (1,)), ((), ())), preferred_element_type=F32)


def _softmax_update(s, v, m_old, acc_old, mask):
    if mask is not None:
        s = jnp.where(mask, s, -jnp.inf)
    m_new = jnp.maximum(m_old, jnp.max(s, axis=-1, keepdims=True))
    cols = s.shape[1]
    m_wide = jnp.concatenate([m_new] * (cols // LANES), axis=1) if cols >= LANES else m_new[:, :cols]
    p = jnp.exp2(s - m_wide)
    alpha = jnp.exp2(m_old - m_new)
    acc = alpha * acc_old + jnp.dot(p.astype(BF16), v, preferred_element_type=F32)
    return m_new, acc


def _pair_output(acc_even, acc_odd):
    lane = lax.broadcasted_iota(jnp.int32, (1, LANES), 1)
    oe = acc_even * pltpu.roll(1.0 / acc_even, HEAD_DIM, 1)
    oo = acc_odd * pltpu.roll(1.0 / acc_odd, HEAD_DIM, 1)
    return jnp.where(lane < HEAD_DIM, oe, pltpu.roll(oo, HEAD_DIM, 1))


def _attn_kernel(q_ref, qn_ref, k_ref, v_ref, o_ref, s0_ref, s1_ref, acc_ref, m_ref, *, tq, tk):
    qi = pl.program_id(2)
    m_ref[...] = jnp.full(m_ref.shape, -jnp.inf, F32)
    acc_ref[...] = jnp.zeros(acc_ref.shape, F32)

    def keys(j):
        return pl.ds(pl.multiple_of(j * tk, tk), tk)

    def qk(src_ref, j, s_ref, r0=0):
        for h in range(2):
            s_ref[h, r0:, :] = _scores(src_ref[0, h, r0:, :], k_ref[0, h, keys(j), :])

    def soft_pv(j, s_ref, mask, r0=0):
        for h in range(2):
            m_new, acc = _softmax_update(s_ref[h, r0:, :], v_ref[0, h, keys(j), :],
                                         m_ref[h, r0:, :], acc_ref[h, r0:, :], mask)
            m_ref[h, r0:, :] = m_new
            acc_ref[h, r0:, :] = acc

    @pl.when(qi == 0)
    def _():
        qk(q_ref, 0, s0_ref)

    def body(u, carry):
        j = 2 * u
        qk(q_ref, j + 1, s1_ref)
        soft_pv(j, s0_ref, None)
        qk(q_ref, j + 2, s0_ref)
        soft_pv(j + 1, s1_ref, None)
        return carry

    lax.fori_loop(0, qi, body, 0)

    row = lax.broadcasted_iota(jnp.int32, (tq, tk), 0)
    col = lax.broadcasted_iota(jnp.int32, (tq, tk), 1)
    qk(q_ref, 2 * qi + 1, s1_ref, tk)
    soft_pv(2 * qi, s0_ref, col <= row)
    qk(qn_ref, 0, s0_ref)
    soft_pv(2 * qi + 1, s1_ref, _tril_ones(tk), tk)

    o_ref[0] = _pair_output(acc_ref[0], acc_ref[1]).astype(o_ref.dtype)


def _attn(qa, ka, va, tk):
    nb, _, s, _ = qa.shape
    tq = 2 * tk
    nq = s // tq
    kv = pl.BlockSpec((1, 2, s, LANES), lambda b, p, i: (b, p, 0, 0))
    return pl.pallas_call(
        functools.partial(_attn_kernel, tq=tq, tk=tk),
        grid=(nb, N_PAIRS, nq),
        in_specs=[pl.BlockSpec((1, 2, tq, LANES), lambda b, p, i: (b, p, i, 0)),
                  pl.BlockSpec((1, 2, tq, LANES), lambda b, p, i: (b, p, jnp.minimum(i + 1, nq - 1), 0)),
                  kv, kv],
        out_specs=pl.BlockSpec((1, tq, LANES), lambda b, p, i: (b, i, p)),
        out_shape=jax.ShapeDtypeStruct((nb, s, WIDTH_A), BF16),
        scratch_shapes=[pltpu.VMEM((2, tq, tk), F32), pltpu.VMEM((2, tq, tk), F32),
                        pltpu.VMEM((2, tq, LANES), F32), pltpu.VMEM((2, tq, LANES), F32)],
        compiler_params=_params(("arbitrary", "arbitrary", "arbitrary")),
        name="fox_prompt_attention",
    )(qa, qa, ka, va)


def _cumsum_lanes(x):
    n = x.shape[1]
    lane = lax.broadcasted_iota(jnp.int32, x.shape, 1)
    k = 1
    while k < n:
        x = x + jnp.where(lane >= k, pltpu.roll(x, k, 1), 0.0)
        k *= 2
    return x


def _attn_sample_kernel(q_ref, k_ref, v_ref, lf_ref, kc_ref, vc_ref, lfc_ref, o_ref, *, t_new):
    lfc = lfc_ref[0]
    c_cache = _cumsum_lanes(lfc)
    p_len = lfc.shape[1]
    tail = (c_cache[:, p_len - 1:p_len] - c_cache) * LOG2E
    c_rel = _cumsum_rows(lf_ref[0][:, :N_HEADS])
    pieces = _split3(c_rel * LOG2E)
    causal = _tril_ones(t_new)
    lane = lax.broadcasted_iota(jnp.int32, (1, LANES), 1)
    lo = lane < HEAD_DIM
    widen = lambda x: jnp.pad(x, ((0, 0), (0, LANES - HEAD_DIM)))
    for p in range(N_PAIRS):
        sl = slice(p * LANES, (p + 1) * LANES)
        he, ho = 2 * p, 2 * p + 1
        q_t = _head_tiles(q_ref[0, :, sl] * (ATTN_SCALE * LOG2E), _aug_q(pieces, he), _aug_q(pieces, ho))
        kn_t = _head_tiles(k_ref[0, :, sl], _aug_k(pieces, he), _aug_k(pieces, ho))
        vn_t = _head_tiles(v_ref[0, :, sl], 1.0, 1.0)
        res = []
        for j, h in enumerate((he, ho)):
            q = q_t[j].astype(BF16)
            s_c = jnp.dot(q[:, :HEAD_DIM], kc_ref[0, h].astype(BF16), preferred_element_type=F32)
            s_c = s_c + (c_rel[:, h:h + 1] * LOG2E + tail[h:h + 1, :])
            s_n = jnp.where(causal, _scores(q, kn_t[j].astype(BF16)), -jnp.inf)
            m = jnp.maximum(jnp.max(s_c, axis=-1, keepdims=True), jnp.max(s_n, axis=-1, keepdims=True))
            p_c = jnp.exp2(s_c - m)
            p_n = jnp.exp2(s_n - m)
            o_c = lax.dot_general(p_c.astype(BF16), vc_ref[0, h].astype(BF16), (((1,), (1,)), ((), ())),
                                  preferred_element_type=F32)
            acc = jnp.dot(p_n.astype(BF16), vn_t[j].astype(BF16), preferred_element_type=F32)
            acc = acc + jnp.where(lo, widen(o_c), jnp.sum(p_c, axis=-1, keepdims=True))
            res.append(acc)
        o_ref[0, :, sl] = _pair_output(res[0], res[1]).astype(o_ref.dtype)


def _attn_sample(q, k, v, lf, kc_t, vc_t, lfc_t):
    nb, t_new, _ = q.shape
    p_len = kc_t.shape[-1]
    new = lambda w: pl.BlockSpec((1, t_new, w), lambda b: (b, 0, 0))
    cache = pl.BlockSpec((1, N_HEADS, HEAD_DIM, p_len), lambda b: (b, 0, 0, 0))
    return pl.pallas_call(
        functools.partial(_attn_sample_kernel, t_new=t_new),
        grid=(nb,),
        in_specs=[new(WIDTH_A), new(WIDTH_A), new(WIDTH_A), new(LANES), cache, cache,
                  pl.BlockSpec((1, N_HEADS, p_len), lambda b: (b, 0, 0))],
        out_specs=new(WIDTH_A),
        out_shape=jax.ShapeDtypeStruct((nb, t_new, WIDTH_A), BF16),
        compiler_params=_params(("arbitrary",)),
        name="fox_sample_attention",
    )(q, k, v, lf, kc_t, vc_t, lfc_t)


def _pad_lanes(v, fill=0.0):
    v = v.reshape(1, -1)
    return jnp.pad(v, ((0, 0), (0, LANES - v.shape[1])), constant_values=fill)


def _layer(xp, xs, mp, ms, cache_k, cache_v, cache_lf, g1, wg1, wu1, wd1, gm, w_in, b_forget, g_q, g_k, g_v,
           w_spatial, b_spatial, w_proj_a, w_proj_b, w_out, g2, wg2, wu2, wd2, *, tm, t_attn):
    nb_p, s_p, _ = xp.shape
    nb_s, t_new, _ = xs.shape
    rows_s = nb_s * t_new
    mp = [m[:, None, :] for m in mp]
    ms = [m[:, None, :] for m in ms]

    wg1, wu1, wd1 = wg1.astype(BF16), wu1.astype(BF16), wd1.astype(BF16)
    wg2, wu2, wd2 = wg2.astype(BF16), wu2.astype(BF16), wd2.astype(BF16)
    o_f = 3 * WIDTH_A
    o_zb = o_f + N_HEADS
    wf = jnp.pad(w_in[:, o_f:o_zb], ((0, 0), (0, LANES - N_HEADS)))
    wcat = jnp.concatenate([w_in[:, :o_f], w_in[:, o_zb:], wf], axis=1).astype(BF16)
    bf = _pad_lanes(b_forget)
    gq = jnp.tile(g_q, 2).reshape(1, LANES)
    gk = jnp.tile(g_k, 2).reshape(1, LANES)
    gv = g_v.reshape(1, WIDTH_B)
    bsp = jnp.repeat(b_spatial.T, GROUP_DIM, axis=1)
    wpa = w_proj_a.astype(BF16)
    wpb = w_proj_b.astype(BF16)
    wo = w_out.astype(BF16)

    xp = _ffn(xp, mp[0], mp[1], mp[2], g1, wg1, wu1, wd1, tm, 1)
    xs = _ffn(xs, ms[0], ms[1], ms[2], g1, wg1, wu1, wd1, t_new, nb_s)

    qa, ka, va, k_p, v_p, lf_p, gas, mb = _mix(
        xp, mp[3], mp[4], gm, wcat, bf, gq, gk, gv, w_spatial, bsp, wpb,
        rows=tm, nbk=1, mix_len=GMLP_CHUNK, mix_period=GMLP_CHUNK, prompt=True)
    a = _attn(qa, ka, va, t_attn)

    wsp_s = jnp.tile(w_spatial[:, :t_new, :t_new], (1, nb_s, nb_s))
    bsp_s = jnp.tile(bsp[:t_new], (nb_s, 1))
    q_s, k_s, v_s, lf_s, vbn_s, gas_s, mb_s = _mix(
        xs, ms[3], ms[4], gm, wcat, bf, gq, gk, gv, wsp_s, bsp_s, wpb,
        rows=t_new, nbk=nb_s, mix_len=rows_s, mix_period=t_new, prompt=False)
    per_req = lambda z: z.reshape(nb_s, t_new, z.shape[-1])
    a_s = _attn_sample(per_req(q_s), per_req(k_s), per_req(v_s), per_req(lf_s),
                       jnp.transpose(cache_k, (0, 2, 3, 1)), jnp.transpose(cache_v, (0, 2, 3, 1)),
                       jnp.transpose(cache_lf, (0, 2, 1)))

    xp = _ffn(xp, mp[6], mp[7], mp[8], g2, wg2, wu2, wd2, tm, 1, merge=(a, gas, mb, mp[5], wpa, wo))
    xs = _ffn(xs, ms[6], ms[7], ms[8], g2, wg2, wu2, wd2, t_new, nb_s,
              merge=(a_s, per_req(gas_s), per_req(mb_s), ms[5], wpa, wo))

    heads = lambda z, nb, s: z.reshape(nb, s, N_HEADS, HEAD_DIM)
    outs = (heads(k_p, nb_p, s_p), heads(v_p, nb_p, s_p), jnp.transpose(lf_p, (0, 2, 1)),
            heads(per_req(k_s), nb_s, t_new), heads(per_req(v_s), nb_s, t_new),
            per_req(lf_s)[:, :, :N_HEADS], per_req(vbn_s))
    return xp, xs, outs


def kernel(x_prompt, x_sample, c_prompt, c_sample, cache_fox_k, cache_fox_v, cache_fox_logf, w_ada, b_ada, g_norm_ffn1, w_ffn1_gate, w_ffn1_up, w_ffn1_down, g_norm_mix, w_in, b_forget, g_q, g_k, g_gmlp_v, w_spatial, b_spatial, w_proj_a, w_proj_b, w_out, g_norm_ffn2, w_ffn2_gate, w_ffn2_up, w_ffn2_down):
    depth = w_ada.shape[0]
    nb_p, s_p, _ = x_prompt.shape
    nb_s = x_sample.shape[0]
    tm = min(512, s_p)
    t_attn = min(512, s_p // 2)
    xp, xs = x_prompt, x_sample
    c_all = jnp.concatenate([c_prompt, c_sample], axis=0)
    rows = c_all.shape[0]
    c_all = jnp.pad(c_all, ((0, -rows % 8), (0, 0)))
    stacks = [[] for _ in range(7)]
    for l in range(depth):
        mod = _ada(c_all, w_ada[l], b_ada[l])
        mp = [mod[:nb_p, i * D_MODEL:(i + 1) * D_MODEL] for i in range(N_MOD)]
        ms = [mod[nb_p:nb_p + nb_s, i * D_MODEL:(i + 1) * D_MODEL] for i in range(N_MOD)]
        xp, xs, outs = _layer(
            xp, xs, mp, ms, cache_fox_k[l], cache_fox_v[l], cache_fox_logf[l],
            g_norm_ffn1[l], w_ffn1_gate[l], w_ffn1_up[l], w_ffn1_down[l],
            g_norm_mix[l], w_in[l], b_forget[l], g_q[l], g_k[l], g_gmlp_v[l], w_spatial[l], b_spatial[l],
            w_proj_a[l], w_proj_b[l], w_out[l],
            g_norm_ffn2[l], w_ffn2_gate[l], w_ffn2_up[l], w_ffn2_down[l], tm=tm, t_attn=t_attn)
        for st, o in zip(stacks, outs):
            st.append(o)
    return (xp, xs) + tuple(jnp.stack(st) for st in stacks)
```

```python
import functools

import jax
import jax.numpy as jnp
from jax import lax
from jax.experimental import pallas as pl
from jax.experimental.pallas import tpu as pltpu

D_MODEL = 1024
N_HEADS = 8
HEAD_DIM = 64
WIDTH_A = N_HEADS * HEAD_DIM
N_GROUPS = 4
GROUP_DIM = 128
WIDTH_B = N_GROUPS * GROUP_DIM
GMLP_CHUNK = 128
D_FF = 2816
N_MOD = 9
EPS = 1e-6
ATTN_SCALE = HEAD_DIM ** -0.5
LOG2E = 1.4426950408889634

LANES = 128
SUBLANES = 8
N_PAIRS = N_HEADS // 2
VMEM_LIMIT = 56 * 1024 * 1024
VMEM_LIMIT_ATTN = 60000 * 1024

F32 = jnp.float32
BF16 = jnp.bfloat16


def _const_spec(shape):
    nd = len(shape)
    return pl.BlockSpec(shape, lambda *_: (0,) * nd, pipeline_mode=pl.Buffered(1))


def _params(sem, vmem_limit=VMEM_LIMIT):
    return pltpu.CompilerParams(dimension_semantics=sem, vmem_limit_bytes=vmem_limit)


def _rms(x):
    return x * lax.rsqrt(jnp.mean(x * x, axis=-1, keepdims=True) + EPS)


def _split3(c):
    hi = c.astype(BF16).astype(F32)
    r = c - hi
    mid = r.astype(BF16).astype(F32)
    lo = (r - mid).astype(BF16).astype(F32)
    return hi, mid, lo


def _tril_ones(n, period=None):
    row = lax.broadcasted_iota(jnp.int32, (n, n), 0)
    col = lax.broadcasted_iota(jnp.int32, (n, n), 1)
    keep = col <= row
    if period is not None and period != n:
        keep = keep & ((row // period) == (col // period))
    return keep


def _cumsum_rows(x):
    n, width = x.shape
    groups = n // SUBLANES
    x3 = x.reshape(groups, SUBLANES, width)
    sub = lax.broadcasted_iota(jnp.int32, (1, SUBLANES, width), 1)
    k = 1
    while k < SUBLANES:
        x3 = x3 + jnp.where(sub >= k, pltpu.roll(x3, k, 1), 0.0)
        k *= 2
    tot = x3[:, SUBLANES - 1:SUBLANES, :]
    k = 1
    while k < groups:
        tot = tot + jnp.concatenate([jnp.zeros((k, 1, width), F32), tot[:groups - k]], axis=0)
        k *= 2
    before = jnp.concatenate([jnp.zeros((1, 1, width), F32), tot[:groups - 1]], axis=0)
    return (x3 + before).reshape(n, width)


def _head_norm(slab, gain):
    lane = lax.broadcasted_iota(jnp.int32, (1, LANES), 1)
    lo = lane < HEAD_DIM
    sq = slab * slab
    ss_lo = jnp.sum(jnp.where(lo, sq, 0.0), axis=-1, keepdims=True)
    ss_hi = jnp.sum(jnp.where(lo, 0.0, sq), axis=-1, keepdims=True)
    r = jnp.where(lo, lax.rsqrt(ss_lo / HEAD_DIM + EPS), lax.rsqrt(ss_hi / HEAD_DIM + EPS))
    return slab * r * gain


def _head_tiles(slab, fill_even, fill_odd):
    lane = lax.broadcasted_iota(jnp.int32, (1, LANES), 1)
    lo = lane < HEAD_DIM
    even = jnp.where(lo, slab, fill_even)
    odd = jnp.where(lo, pltpu.roll(slab, HEAD_DIM, 1), fill_odd)
    return even, odd


def _aug_q(pieces, h):
    lane = lax.broadcasted_iota(jnp.int32, (1, LANES), 1)
    hi, mid, lo = (p[:, h:h + 1] for p in pieces)
    ones = jnp.where((lane >= HEAD_DIM + 3) & (lane < HEAD_DIM + 6), 1.0, 0.0)
    return jnp.where(lane == HEAD_DIM, hi, jnp.where(lane == HEAD_DIM + 1, mid,
                     jnp.where(lane == HEAD_DIM + 2, lo, ones)))


def _aug_k(pieces, h):
    lane = lax.broadcasted_iota(jnp.int32, (1, LANES), 1)
    hi, mid, lo = (-p[:, h:h + 1] for p in pieces)
    ones = jnp.where((lane >= HEAD_DIM) & (lane < HEAD_DIM + 3), 1.0, 0.0)
    return jnp.where(lane == HEAD_DIM + 3, hi, jnp.where(lane == HEAD_DIM + 4, mid,
                     jnp.where(lane == HEAD_DIM + 5, lo, ones)))


def _log_sigmoid(x):
    return jnp.minimum(x, 0.0) - jnp.log1p(jnp.exp(-jnp.abs(x)))


def _ada_kernel(c_ref, w_ref, b_ref, o_ref):
    h = jax.nn.silu(c_ref[...]).astype(BF16)
    o_ref[...] = jnp.dot(h, w_ref[...].astype(BF16), preferred_element_type=F32) + b_ref[...]


def _ada(c, w_ada, b_ada):
    rows = c.shape[0]
    n = w_ada.shape[1]
    tn = D_MODEL
    return pl.pallas_call(
        _ada_kernel,
        grid=(n // tn,),
        in_specs=[pl.BlockSpec((rows, D_MODEL), lambda j: (0, 0)),
                  pl.BlockSpec((D_MODEL, tn), lambda j: (0, j)),
                  pl.BlockSpec((1, tn), lambda j: (0, j))],
        out_specs=pl.BlockSpec((rows, tn), lambda j: (0, j)),
        out_shape=jax.ShapeDtypeStruct((rows, n), F32),
        compiler_params=_params(("arbitrary",)),
        name="ada_modulation",
    )(c, w_ada, b_ada.reshape(1, n))


def _flat(x3):
    return x3.reshape(x3.shape[0] * x3.shape[1], x3.shape[2])


def _merge_rows(a_ref, gas_ref, mb_ref, wpa_ref, wo_ref):
    pa = jnp.dot(_flat(a_ref[...]), wpa_ref[...], preferred_element_type=F32)
    m = _flat(gas_ref[...]).astype(F32) * pa + _flat(mb_ref[...]).astype(F32)
    return jnp.dot(m.astype(BF16), wo_ref[...], preferred_element_type=F32)


def _ffn_kernel(*refs, merge):
    if merge:
        (x_ref, a_ref, gas_ref, mb_ref, mgate_ref, wpa_ref, wo_ref,
         shift_ref, scale_ref, gate_ref, g_ref, wg_ref, wu_ref, wd_ref, o_ref) = refs
        x3 = x_ref[...]
        x3 = x3 + mgate_ref[...] * _merge_rows(a_ref, gas_ref, mb_ref, wpa_ref, wo_ref).reshape(x3.shape)
    else:
        x_ref, shift_ref, scale_ref, gate_ref, g_ref, wg_ref, wu_ref, wd_ref, o_ref = refs
        x3 = x_ref[...]
    h = _flat((_rms(x3) * g_ref[...]) * (1.0 + scale_ref[...]) + shift_ref[...]).astype(BF16)
    a = jax.nn.silu(jnp.dot(h, wg_ref[...], preferred_element_type=F32))
    a = a * jnp.dot(h, wu_ref[...], preferred_element_type=F32)
    y = jnp.dot(a.astype(BF16), wd_ref[...], preferred_element_type=F32)
    o_ref[...] = x3 + 0.5 * gate_ref[...] * y.reshape(x3.shape)


def _row_specs(nbk, tm):
    row = lambda w: pl.BlockSpec((nbk, tm, w), lambda b, i: (b, i, 0))
    mod = pl.BlockSpec((nbk, 1, D_MODEL), lambda b, i: (b, 0, 0))
    return row, mod


def _ffn(x, shift, scale, gate, g, wg, wu, wd, tm, nbk, merge=None):
    nb, s, _ = x.shape
    row, mod = _row_specs(nbk, tm)
    weights = [_const_spec((1, 1, D_MODEL)), _const_spec(wg.shape), _const_spec(wu.shape), _const_spec(wd.shape)]
    args = [shift, scale, gate, g.reshape(1, 1, D_MODEL), wg, wu, wd]
    in_specs = [mod, mod, mod] + weights
    if merge is not None:
        a, gas, mb, mgate, wpa, wo = merge
        args = [a, gas, mb, mgate, wpa, wo] + args
        in_specs = [row(WIDTH_A), row(D_MODEL), row(D_MODEL), mod, _const_spec(wpa.shape),
                    _const_spec(wo.shape)] + in_specs
    return pl.pallas_call(
        functools.partial(_ffn_kernel, merge=merge is not None),
        grid=(nb // nbk, s // tm),
        in_specs=[row(D_MODEL)] + in_specs,
        out_specs=row(D_MODEL),
        out_shape=jax.ShapeDtypeStruct(x.shape, F32),
        compiler_params=_params(("arbitrary", "arbitrary")),
        name="ffn_half_step_merged" if merge is not None else "ffn_half_step",
    )(x, *args)


def _mix_project(x_ref, shift_ref, scale_ref, g_ref, wcat_ref, z_ref):
    n = _flat((_rms(x_ref[...]) * g_ref[...]) * (1.0 + scale_ref[...]) + shift_ref[...])
    z_ref[...] = jnp.dot(n.astype(BF16), wcat_ref[...], preferred_element_type=F32)


def _mix_finish(z_ref, first_tile, bf_ref, gq_ref, gk_ref, gv_ref, wsp_ref, bsp_ref, wpb_ref, outs,
                *, tm, mix_len, mix_period, prompt):
    if prompt:
        qa_ref, ka_ref, va_ref, ko_ref, vo_ref, lf_ref, gas_ref, mb_ref, carry_ref = outs
    else:
        qo_ref, ko_ref, vo_ref, lf_ref, vbn_ref, gas_ref, mb_ref = outs
    o_q, o_k, o_v = 0, WIDTH_A, 2 * WIDTH_A
    o_zb = 3 * WIDTH_A
    o_ga = o_zb + 2 * WIDTH_B
    o_gb = o_ga + D_MODEL
    o_f = o_gb + D_MODEL

    lane = lax.broadcasted_iota(jnp.int32, (1, LANES), 1)
    logf = jnp.where(lane < N_HEADS, _log_sigmoid(z_ref[:, o_f:o_f + LANES] + bf_ref[...]), 0.0)
    if prompt:
        lf_ref[0] = logf.T[:N_HEADS, :]
    else:
        lf_ref[0] = logf

    if prompt:
        c = _cumsum_rows(logf) + jnp.where(first_tile, 0.0, carry_ref[...])
        carry_ref[...] = c[tm - 1:tm, :]
        pieces = _split3(c * LOG2E)

    for p in range(N_PAIRS):
        sl = slice(p * LANES, (p + 1) * LANES)
        qn = _head_norm(z_ref[:, o_q + p * LANES:o_q + (p + 1) * LANES], gq_ref[...])
        kn = _head_norm(z_ref[:, o_k + p * LANES:o_k + (p + 1) * LANES], gk_ref[...])
        v = z_ref[:, o_v + p * LANES:o_v + (p + 1) * LANES]
        ko_ref[0, :, sl] = kn
        vo_ref[0, :, sl] = v
        if prompt:
            he, ho = 2 * p, 2 * p + 1
            qe, qo = _head_tiles(qn * (ATTN_SCALE * LOG2E), _aug_q(pieces, he), _aug_q(pieces, ho))
            ke, ko = _head_tiles(kn, _aug_k(pieces, he), _aug_k(pieces, ho))
            ve, vo = _head_tiles(v, 1.0, 1.0)
            qa_ref[0, he] = qe.astype(BF16)
            qa_ref[0, ho] = qo.astype(BF16)
            ka_ref[0, he] = ke.astype(BF16)
            ka_ref[0, ho] = ko.astype(BF16)
            va_ref[0, he] = ve.astype(BF16)
            va_ref[0, ho] = vo.astype(BF16)
        else:
            qo_ref[0, :, sl] = qn

    zb = jax.nn.gelu(z_ref[:, o_zb:o_zb + 2 * WIDTH_B])
    u = zb[:, :WIDTH_B]
    vbn = _rms(zb[:, WIDTH_B:]) * gv_ref[...]
    if not prompt:
        vbn_ref[0] = vbn
    vbb = vbn.astype(BF16)
    keep = _tril_ones(mix_len, mix_period)
    wm = [jnp.where(keep, wsp_ref[g], 0.0).astype(BF16) for g in range(N_GROUPS)]
    rows = []
    for c0 in range(0, tm, mix_len):
        cols = [jnp.dot(wm[g], vbb[c0:c0 + mix_len, g * GROUP_DIM:(g + 1) * GROUP_DIM],
                        preferred_element_type=F32) for g in range(N_GROUPS)]
        rows.append(jnp.concatenate(cols, axis=1) + bsp_ref[...])
    mixed = rows[0] if len(rows) == 1 else jnp.concatenate(rows, axis=0)
    b = (u * mixed).astype(BF16)
    pb = jnp.dot(b, wpb_ref[...], preferred_element_type=F32)
    mb_ref[0] = (jax.nn.sigmoid(z_ref[:, o_gb:o_gb + D_MODEL]) * pb).astype(BF16)
    gas_ref[0] = jax.nn.sigmoid(z_ref[:, o_ga:o_ga + D_MODEL]).astype(BF16)


def _mix_kernel(x_ref, shift_ref, scale_ref, g_ref, wcat_ref, bf_ref, gq_ref, gk_ref, gv_ref,
                wsp_ref, bsp_ref, wpb_ref, *rest, n_tiles, **static):
    n_z = min(n_tiles, 2)
    z_refs = rest[-n_z:]
    outs = rest[:-n_z]
    project = functools.partial(_mix_project, x_ref, shift_ref, scale_ref, g_ref, wcat_ref)
    finish = functools.partial(_mix_finish, bf_ref=bf_ref, gq_ref=gq_ref, gk_ref=gk_ref, gv_ref=gv_ref,
                               wsp_ref=wsp_ref, bsp_ref=bsp_ref, wpb_ref=wpb_ref, outs=outs, **static)
    i = pl.program_id(1)
    if n_tiles == 1:
        project(z_refs[0])
        finish(z_refs[0], True)
        return

    @pl.when(i == 0)
    def _():
        project(z_refs[0])

    for parity in range(2):
        @pl.when((i > 0) & (i < n_tiles) & (i % 2 == parity))
        def _():
            project(z_refs[parity])
            finish(z_refs[1 - parity], i == 1)

    @pl.when(i == n_tiles)
    def _():
        finish(z_refs[(n_tiles - 1) % 2], n_tiles == 1)


def _mix(x, shift, scale, g, wcat, bf, gq, gk, gv, wsp, bsp, wpb, *, rows, nbk, mix_len, mix_period, prompt):
    nb, s, _ = x.shape
    tm = nbk * rows
    n_tiles = s // rows
    steps = n_tiles if n_tiles == 1 else n_tiles + 1
    done = lambda i: jnp.maximum(i - 1, 0) if n_tiles > 1 else i
    row = lambda w: pl.BlockSpec((1, tm, w), lambda b, i: (b, done(i), 0))
    head = pl.BlockSpec((1, N_HEADS, tm, LANES), lambda b, i: (b, 0, done(i), 0))
    x_spec = pl.BlockSpec((nbk, rows, D_MODEL), lambda b, i: (b, jnp.minimum(i, n_tiles - 1), 0))
    mod = pl.BlockSpec((nbk, 1, D_MODEL), lambda b, i: (b, 0, 0))
    sds = jax.ShapeDtypeStruct
    tok = lambda w, dt: sds((nb // nbk, nbk * s, w), dt)
    hm = sds((nb // nbk, N_HEADS, nbk * s, LANES), BF16)
    if prompt:
        lf_spec = pl.BlockSpec((1, N_HEADS, tm), lambda b, i: (b, 0, done(i)))
        out_specs = [head, head, head, row(WIDTH_A), row(WIDTH_A), lf_spec, row(D_MODEL), row(D_MODEL)]
        out_shape = [hm, hm, hm, tok(WIDTH_A, F32), tok(WIDTH_A, F32), sds((nb, N_HEADS, s), F32),
                     tok(D_MODEL, BF16), tok(D_MODEL, BF16)]
        scratch = [pltpu.VMEM((1, LANES), F32)]
    else:
        out_specs = [row(WIDTH_A), row(WIDTH_A), row(WIDTH_A), row(LANES), row(WIDTH_B),
                     row(D_MODEL), row(D_MODEL)]
        out_shape = [tok(WIDTH_A, F32), tok(WIDTH_A, F32), tok(WIDTH_A, F32), tok(LANES, F32),
                     tok(WIDTH_B, F32), tok(D_MODEL, BF16), tok(D_MODEL, BF16)]
        scratch = []
    scratch += [pltpu.VMEM((tm, wcat.shape[1]), F32)] * min(n_tiles, 2)
    kern = functools.partial(_mix_kernel, n_tiles=n_tiles, tm=tm, mix_len=mix_len, mix_period=mix_period,
                             prompt=prompt)
    return pl.pallas_call(
        kern,
        grid=(nb // nbk, steps),
        in_specs=[x_spec, mod, mod, _const_spec((1, 1, D_MODEL)),
                  _const_spec(wcat.shape), _const_spec(bf.shape), _const_spec(gq.shape),
                  _const_spec(gk.shape), _const_spec(gv.shape), _const_spec(wsp.shape),
                  _const_spec(bsp.shape), _const_spec(wpb.shape)],
        out_specs=out_specs,
        out_shape=out_shape,
        scratch_shapes=scratch,
        compiler_params=_params(("arbitrary", "arbitrary")),
        name="mixer_in_prompt" if prompt else "mixer_in_sample",
    )(x, shift, scale, g.reshape(1, 1, D_MODEL), wcat, bf, gq, gk, gv, wsp, bsp, wpb)


def _scores(q, k):
    return lax.dot_general(q, k, (((1,), (1,)), ((), ())), preferred_element_type=F32)


def _softmax_update(s, v, m_old, acc_old, mask):
    if mask is not None:
        s = jnp.where(mask, s, -jnp.inf)
    m_new = jnp.maximum(m_old, jnp.max(s, axis=-1, keepdims=True))
    cols = s.shape[1]
    m_wide = jnp.concatenate([m_new] * (cols // LANES), axis=1) if cols >= LANES else m_new[:, :cols]
    p = jnp.exp2(s - m_wide)
    alpha = jnp.exp2(m_old - m_new)
    acc = alpha * acc_old + jnp.dot(p.astype(BF16), v, preferred_element_type=F32)
    return m_new, acc


def _pair_output(acc_even, acc_odd):
    lane = lax.broadcasted_iota(jnp.int32, (1, LANES), 1)
    oe = acc_even * pltpu.roll(1.0 / acc_even, HEAD_DIM, 1)
    oo = acc_odd * pltpu.roll(1.0 / acc_odd, HEAD_DIM, 1)
    return jnp.where(lane < HEAD_DIM, oe, pltpu.roll(oo, HEAD_DIM, 1))


def _attn_kernel(q_ref, qn_ref, k_ref, v_ref, o_ref, s0_ref, s1_ref, acc_ref, m_ref, *, tq, tk):
    qi = pl.program_id(2)
    m_ref[...] = jnp.full(m_ref.shape, -jnp.inf, F32)
    acc_ref[...] = jnp.zeros(acc_ref.shape, F32)

    def keys(j):
        return pl.ds(pl.multiple_of(j * tk, tk), tk)

    def qk(src_ref, j, s_ref, r0=0):
        for h in range(2):
            s_ref[h, r0:, :] = _scores(src_ref[0, h, r0:, :], k_ref[0, h, keys(j), :])

    def soft_pv(j, s_ref, mask, r0=0):
        for h in range(2):
            m_new, acc = _softmax_update(s_ref[h, r0:, :], v_ref[0, h, keys(j), :],
                                         m_ref[h, r0:, :], acc_ref[h, r0:, :], mask)
            m_ref[h, r0:, :] = m_new
            acc_ref[h, r0:, :] = acc

    @pl.when(qi == 0)
    def _():
        qk(q_ref, 0, s0_ref)

    def body(u, carry):
        j = 2 * u
        qk(q_ref, j + 1, s1_ref)
        soft_pv(j, s0_ref, None)
        qk(q_ref, j + 2, s0_ref)
        soft_pv(j + 1, s1_ref, None)
        return carry

    def body_pair(w, carry):
        body(2 * w, carry)
        return body(2 * w + 1, carry)

    lax.fori_loop(0, qi // 2, body_pair, 0)

    @pl.when(qi % 2 == 1)
    def _():
        body(qi - 1, 0)

    row = lax.broadcasted_iota(jnp.int32, (tq, tk), 0)
    col = lax.broadcasted_iota(jnp.int32, (tq, tk), 1)
    qk(q_ref, 2 * qi + 1, s1_ref, tk)
    soft_pv(2 * qi, s0_ref, col <= row)
    qk(qn_ref, 0, s0_ref)
    soft_pv(2 * qi + 1, s1_ref, _tril_ones(tk), tk)

    o_ref[0] = _pair_output(acc_ref[0], acc_ref[1]).astype(o_ref.dtype)


def _attn(qa, ka, va, tk):
    nb, _, s, _ = qa.shape
    tq = 2 * tk
    nq = s // tq
    kv = pl.BlockSpec((1, 2, s, LANES), lambda b, p, i: (b, p, 0, 0))
    return pl.pallas_call(
        functools.partial(_attn_kernel, tq=tq, tk=tk),
        grid=(nb, N_PAIRS, nq),
        in_specs=[pl.BlockSpec((1, 2, tq, LANES), lambda b, p, i: (b, p, i, 0)),
                  pl.BlockSpec((1, 2, tq, LANES), lambda b, p, i: (b, p, jnp.minimum(i + 1, nq - 1), 0)),
                  kv, kv],
        out_specs=pl.BlockSpec((1, tq, LANES), lambda b, p, i: (b, i, p)),
        out_shape=jax.ShapeDtypeStruct((nb, s, WIDTH_A), BF16),
        scratch_shapes=[pltpu.VMEM((2, tq, tk), F32), pltpu.VMEM((2, tq, tk), F32),
                        pltpu.VMEM((2, tq, LANES), F32), pltpu.VMEM((2, tq, LANES), F32)],
        compiler_params=_params(("arbitrary", "arbitrary", "arbitrary"), VMEM_LIMIT_ATTN),
        name="fox_prompt_attention",
    )(qa, qa, ka, va)


def _cumsum_lanes(x):
    n = x.shape[1]
    lane = lax.broadcasted_iota(jnp.int32, x.shape, 1)
    k = 1
    while k < n:
        x = x + jnp.where(lane >= k, pltpu.roll(x, k, 1), 0.0)
        k *= 2
    return x


def _attn_sample_kernel(q_ref, k_ref, v_ref, lf_ref, kc_ref, vc_ref, lfc_ref, o_ref, *, t_new):
    lfc = lfc_ref[0]
    c_cache = _cumsum_lanes(lfc)
    p_len = lfc.shape[1]
    tail = (c_cache[:, p_len - 1:p_len] - c_cache) * LOG2E
    c_rel = _cumsum_rows(lf_ref[0][:, :N_HEADS])
    pieces = _split3(c_rel * LOG2E)
    causal = _tril_ones(t_new)
    lane = lax.broadcasted_iota(jnp.int32, (1, LANES), 1)
    lo = lane < HEAD_DIM
    widen = lambda x: jnp.pad(x, ((0, 0), (0, LANES - HEAD_DIM)))
    for p in range(N_PAIRS):
        sl = slice(p * LANES, (p + 1) * LANES)
        he, ho = 2 * p, 2 * p + 1
        q_t = _head_tiles(q_ref[0, :, sl] * (ATTN_SCALE * LOG2E), _aug_q(pieces, he), _aug_q(pieces, ho))
        kn_t = _head_tiles(k_ref[0, :, sl], _aug_k(pieces, he), _aug_k(pieces, ho))
        vn_t = _head_tiles(v_ref[0, :, sl], 1.0, 1.0)
        res = []
        for j, h in enumerate((he, ho)):
            q = q_t[j].astype(BF16)
            s_c = jnp.dot(q[:, :HEAD_DIM], kc_ref[0, h].astype(BF16), preferred_element_type=F32)
            s_c = s_c + (c_rel[:, h:h + 1] * LOG2E + tail[h:h + 1, :])
            s_n = jnp.where(causal, _scores(q, kn_t[j].astype(BF16)), -jnp.inf)
            m = jnp.maximum(jnp.max(s_c, axis=-1, keepdims=True), jnp.max(s_n, axis=-1, keepdims=True))
            p_c = jnp.exp2(s_c - m)
            p_n = jnp.exp2(s_n - m)
            o_c = lax.dot_general(p_c.astype(BF16), vc_ref[0, h].astype(BF16), (((1,), (1,)), ((), ())),
                                  preferred_element_type=F32)
            acc = jnp.dot(p_n.astype(BF16), vn_t[j].astype(BF16), preferred_element_type=F32)
            acc = acc + jnp.where(lo, widen(o_c), jnp.sum(p_c, axis=-1, keepdims=True))
            res.append(acc)
        o_ref[0, :, sl] = _pair_output(res[0], res[1]).astype(o_ref.dtype)


def _attn_sample(q, k, v, lf, kc_t, vc_t, lfc_t):
    nb, t_new, _ = q.shape
    p_len = kc_t.shape[-1]
    new = lambda w: pl.BlockSpec((1, t_new, w), lambda b: (b, 0, 0))
    cache = pl.BlockSpec((1, N_HEADS, HEAD_DIM, p_len), lambda b: (b, 0, 0, 0))
    return pl.pallas_call(
        functools.partial(_attn_sample_kernel, t_new=t_new),
        grid=(nb,),
        in_specs=[new(WIDTH_A), new(WIDTH_A), new(WIDTH_A), new(LANES), cache, cache,
                  pl.BlockSpec((1, N_HEADS, p_len), lambda b: (b, 0, 0))],
        out_specs=new(WIDTH_A),
        out_shape=jax.ShapeDtypeStruct((nb, t_new, WIDTH_A), BF16),
        compiler_params=_params(("arbitrary",)),
        name="fox_sample_attention",
    )(q, k, v, lf, kc_t, vc_t, lfc_t)


def _pad_lanes(v, fill=0.0):
    v = v.reshape(1, -1)
    return jnp.pad(v, ((0, 0), (0, LANES - v.shape[1])), constant_values=fill)


def _layer(xp, xs, mp, ms, cache_k, cache_v, cache_lf, g1, wg1, wu1, wd1, gm, w_in, b_forget, g_q, g_k, g_v,
           w_spatial, b_spatial, w_proj_a, w_proj_b, w_out, g2, wg2, wu2, wd2, *, tm, t_attn):
    nb_p, s_p, _ = xp.shape
    nb_s, t_new, _ = xs.shape
    rows_s = nb_s * t_new
    mp = [m[:, None, :] for m in mp]
    ms = [m[:, None, :] for m in ms]

    wg1, wu1, wd1 = wg1.astype(BF16), wu1.astype(BF16), wd1.astype(BF16)
    wg2, wu2, wd2 = wg2.astype(BF16), wu2.astype(BF16), wd2.astype(BF16)
    o_f = 3 * WIDTH_A
    o_zb = o_f + N_HEADS
    wf = jnp.pad(w_in[:, o_f:o_zb], ((0, 0), (0, LANES - N_HEADS)))
    wcat = jnp.concatenate([w_in[:, :o_f], w_in[:, o_zb:], wf], axis=1).astype(BF16)
    bf = _pad_lanes(b_forget)
    gq = jnp.tile(g_q, 2).reshape(1, LANES)
    gk = jnp.tile(g_k, 2).reshape(1, LANES)
    gv = g_v.reshape(1, WIDTH_B)
    bsp = jnp.repeat(b_spatial.T, GROUP_DIM, axis=1)
    wpa = w_proj_a.astype(BF16)
    wpb = w_proj_b.astype(BF16)
    wo = w_out.astype(BF16)

    xp = _ffn(xp, mp[0], mp[1], mp[2], g1, wg1, wu1, wd1, tm, 1)
    xs = _ffn(xs, ms[0], ms[1], ms[2], g1, wg1, wu1, wd1, t_new, nb_s)

    qa, ka, va, k_p, v_p, lf_p, gas, mb = _mix(
        xp, mp[3], mp[4], gm, wcat, bf, gq, gk, gv, w_spatial, bsp, wpb,
        rows=tm, nbk=1, mix_len=GMLP_CHUNK, mix_period=GMLP_CHUNK, prompt=True)
    a = _attn(qa, ka, va, t_attn)

    wsp_s = jnp.tile(w_spatial[:, :t_new, :t_new], (1, nb_s, nb_s))
    bsp_s = jnp.tile(bsp[:t_new], (nb_s, 1))
    q_s, k_s, v_s, lf_s, vbn_s, gas_s, mb_s = _mix(
        xs, ms[3], ms[4], gm, wcat, bf, gq, gk, gv, wsp_s, bsp_s, wpb,
        rows=t_new, nbk=nb_s, mix_len=rows_s, mix_period=t_new, prompt=False)
    per_req = lambda z: z.reshape(nb_s, t_new, z.shape[-1])
    a_s = _attn_sample(per_req(q_s), per_req(k_s), per_req(v_s), per_req(lf_s),
                       jnp.transpose(cache_k, (0, 2, 3, 1)), jnp.transpose(cache_v, (0, 2, 3, 1)),
                       jnp.transpose(cache_lf, (0, 2, 1)))

    xp = _ffn(xp, mp[6], mp[7], mp[8], g2, wg2, wu2, wd2, tm, 1, merge=(a, gas, mb, mp[5], wpa, wo))
    xs = _ffn(xs, ms[6], ms[7], ms[8], g2, wg2, wu2, wd2, t_new, nb_s,
              merge=(a_s, per_req(gas_s), per_req(mb_s), ms[5], wpa, wo))

    heads = lambda z, nb, s: z.reshape(nb, s, N_HEADS, HEAD_DIM)
    outs = (heads(k_p, nb_p, s_p), heads(v_p, nb_p, s_p), jnp.transpose(lf_p, (0, 2, 1)),
            heads(per_req(k_s), nb_s, t_new), heads(per_req(v_s), nb_s, t_new),
            per_req(lf_s)[:, :, :N_HEADS], per_req(vbn_s))
    return xp, xs, outs


def kernel(x_prompt, x_sample, c_prompt, c_sample, cache_fox_k, cache_fox_v, cache_fox_logf, w_ada, b_ada, g_norm_ffn1, w_ffn1_gate, w_ffn1_up, w_ffn1_down, g_norm_mix, w_in, b_forget, g_q, g_k, g_gmlp_v, w_spatial, b_spatial, w_proj_a, w_proj_b, w_out, g_norm_ffn2, w_ffn2_gate, w_ffn2_up, w_ffn2_down):
    depth = w_ada.shape[0]
    nb_p, s_p, _ = x_prompt.shape
    nb_s = x_sample.shape[0]
    tm = min(512, s_p)
    t_attn = min(512, s_p // 2)
    xp, xs = x_prompt, x_sample
    c_all = jnp.concatenate([c_prompt, c_sample], axis=0)
    rows = c_all.shape[0]
    c_all = jnp.pad(c_all, ((0, -rows % 8), (0, 0)))
    stacks = [[] for _ in range(7)]
    for l in range(depth):
        mod = _ada(c_all, w_ada[l], b_ada[l])
        mp = [mod[:nb_p, i * D_MODEL:(i + 1) * D_MODEL] for i in range(N_MOD)]
        ms = [mod[nb_p:nb_p + nb_s, i * D_MODEL:(i + 1) * D_MODEL] for i in range(N_MOD)]
        xp, xs, outs = _layer(
            xp, xs, mp, ms, cache_fox_k[l], cache_fox_v[l], cache_fox_logf[l],
            g_norm_ffn1[l], w_ffn1_gate[l], w_ffn1_up[l], w_ffn1_down[l],
            g_norm_mix[l], w_in[l], b_forget[l], g_q[l], g_k[l], g_gmlp_v[l], w_spatial[l], b_spatial[l],
            w_proj_a[l], w_proj_b[l], w_out[l],
            g_norm_ffn2[l], w_ffn2_gate[l], w_ffn2_up[l], w_ffn2_down[l], tm=tm, t_attn=t_attn)
        for st, o in zip(stacks, outs):
            st.append(o)
    return (xp, xs) + tuple(jnp.stack(st) for st in stacks)
```

```python
import functools

import jax
import jax.numpy as jnp
from jax import lax
from jax.experimental import pallas as pl
from jax.experimental.pallas import tpu as pltpu

D_MODEL = 1024
N_HEADS = 8
HEAD_DIM = 64
WIDTH_A = N_HEADS * HEAD_DIM
N_GROUPS = 4
GROUP_DIM = 128
WIDTH_B = N_GROUPS * GROUP_DIM
GMLP_CHUNK = 128
D_FF = 2816
N_MOD = 9
EPS = 1e-6
ATTN_SCALE = HEAD_DIM ** -0.5
LOG2E = 1.4426950408889634

LANES = 128
SUBLANES = 8
N_PAIRS = N_HEADS // 2
VMEM_LIMIT = 56 * 1024 * 1024
VMEM_LIMIT_ATTN = 60000 * 1024

F32 = jnp.float32
BF16 = jnp.bfloat16


def _const_spec(shape):
    nd = len(shape)
    return pl.BlockSpec(shape, lambda *_: (0,) * nd, pipeline_mode=pl.Buffered(1))


def _params(sem, vmem_limit=VMEM_LIMIT):
    return pltpu.CompilerParams(dimension_semantics=sem, vmem_limit_bytes=vmem_limit)


def _rms(x):
    return x * lax.rsqrt(jnp.mean(x * x, axis=-1, keepdims=True) + EPS)


def _split3(c):
    hi = c.astype(BF16).astype(F32)
    r = c - hi
    mid = r.astype(BF16).astype(F32)
    lo = (r - mid).astype(BF16).astype(F32)
    return hi, mid, lo


def _tril_ones(n, period=None):
    row = lax.broadcasted_iota(jnp.int32, (n, n), 0)
    col = lax.broadcasted_iota(jnp.int32, (n, n), 1)
    keep = col <= row
    if period is not None and period != n:
        keep = keep & ((row // period) == (col // period))
    return keep


def _cumsum_rows(x):
    n, width = x.shape
    groups = n // SUBLANES
    x3 = x.reshape(groups, SUBLANES, width)
    sub = lax.broadcasted_iota(jnp.int32, (1, SUBLANES, width), 1)
    k = 1
    while k < SUBLANES:
        x3 = x3 + jnp.where(sub >= k, pltpu.roll(x3, k, 1), 0.0)
        k *= 2
    tot = x3[:, SUBLANES - 1:SUBLANES, :]
    k = 1
    while k < groups:
        tot = tot + jnp.concatenate([jnp.zeros((k, 1, width), F32), tot[:groups - k]], axis=0)
        k *= 2
    before = jnp.concatenate([jnp.zeros((1, 1, width), F32), tot[:groups - 1]], axis=0)
    return (x3 + before).reshape(n, width)


def _head_norm(slab, gain):
    lane = lax.broadcasted_iota(jnp.int32, (1, LANES), 1)
    lo = lane < HEAD_DIM
    sq = slab * slab
    ss_lo = jnp.sum(jnp.where(lo, sq, 0.0), axis=-1, keepdims=True)
    ss_hi = jnp.sum(jnp.where(lo, 0.0, sq), axis=-1, keepdims=True)
    r = jnp.where(lo, lax.rsqrt(ss_lo / HEAD_DIM + EPS), lax.rsqrt(ss_hi / HEAD_DIM + EPS))
    return slab * r * gain


def _head_tiles(slab, fill_even, fill_odd):
    lane = lax.broadcasted_iota(jnp.int32, (1, LANES), 1)
    lo = lane < HEAD_DIM
    even = jnp.where(lo, slab, fill_even)
    odd = jnp.where(lo, pltpu.roll(slab, HEAD_DIM, 1), fill_odd)
    return even, odd


def _aug_q(pieces, h):
    lane = lax.broadcasted_iota(jnp.int32, (1, LANES), 1)
    hi, mid, lo = (p[:, h:h + 1] for p in pieces)
    ones = jnp.where((lane >= HEAD_DIM + 3) & (lane < HEAD_DIM + 6), 1.0, 0.0)
    return jnp.where(lane == HEAD_DIM, hi, jnp.where(lane == HEAD_DIM + 1, mid,
                     jnp.where(lane == HEAD_DIM + 2, lo, ones)))


def _aug_k(pieces, h):
    lane = lax.broadcasted_iota(jnp.int32, (1, LANES), 1)
    hi, mid, lo = (-p[:, h:h + 1] for p in pieces)
    ones = jnp.where((lane >= HEAD_DIM) & (lane < HEAD_DIM + 3), 1.0, 0.0)
    return jnp.where(lane == HEAD_DIM + 3, hi, jnp.where(lane == HEAD_DIM + 4, mid,
                     jnp.where(lane == HEAD_DIM + 5, lo, ones)))


def _sigmoid(x):
    return 0.5 * jnp.tanh(0.5 * x) + 0.5


def _log_sigmoid(x):
    return jnp.minimum(x, 0.0) - jnp.log1p(jnp.exp(-jnp.abs(x)))


def _ada_kernel(c_ref, w_ref, b_ref, o_ref):
    h = jax.nn.silu(c_ref[...]).astype(BF16)
    o_ref[0] = jnp.dot(h, w_ref[...].astype(BF16), preferred_element_type=F32) + b_ref[...]


def _ada(c, w_ada, b_ada):
    rows = c.shape[0]
    n = w_ada.shape[1]
    tn = D_MODEL
    return pl.pallas_call(
        _ada_kernel,
        grid=(n // tn,),
        in_specs=[pl.BlockSpec((rows, D_MODEL), lambda j: (0, 0)),
                  pl.BlockSpec((D_MODEL, tn), lambda j: (0, j)),
                  pl.BlockSpec((1, tn), lambda j: (0, j))],
        out_specs=pl.BlockSpec((1, rows, tn), lambda j: (j, 0, 0)),
        out_shape=jax.ShapeDtypeStruct((n // tn, rows, tn), F32),
        compiler_params=_params(("arbitrary",)),
        name="ada_modulation",
    )(c, w_ada, b_ada.reshape(1, n))


def _flat(x3):
    return x3.reshape(x3.shape[0] * x3.shape[1], x3.shape[2])


def _merge_rows(a_ref, gas_ref, mb_ref, wpa_ref, wo_ref):
    pa = jnp.dot(_flat(a_ref[...]), wpa_ref[...], preferred_element_type=F32)
    m = _flat(gas_ref[...]).astype(F32) * pa + _flat(mb_ref[...]).astype(F32)
    return jnp.dot(m.astype(BF16), wo_ref[...], preferred_element_type=F32)


def _ffn_kernel(*refs, merge):
    if merge:
        (x_ref, a_ref, gas_ref, mb_ref, mgate_ref, wpa_ref, wo_ref,
         shift_ref, scale_ref, gate_ref, g_ref, wg_ref, wu_ref, wd_ref, o_ref) = refs
        x3 = x_ref[...]
        x3 = x3 + mgate_ref[0] * _merge_rows(a_ref, gas_ref, mb_ref, wpa_ref, wo_ref).reshape(x3.shape)
    else:
        x_ref, shift_ref, scale_ref, gate_ref, g_ref, wg_ref, wu_ref, wd_ref, o_ref = refs
        x3 = x_ref[...]
    h = _flat((_rms(x3) * g_ref[...]) * (1.0 + scale_ref[0]) + shift_ref[0]).astype(BF16)
    a = jax.nn.silu(jnp.dot(h, wg_ref[...], preferred_element_type=F32))
    a = a * jnp.dot(h, wu_ref[...], preferred_element_type=F32)
    y = jnp.dot(a.astype(BF16), wd_ref[...], preferred_element_type=F32)
    o_ref[...] = x3 + 0.5 * gate_ref[0] * y.reshape(x3.shape)


def _mod_spec(k, nbk, row0):
    return pl.BlockSpec((1, nbk, 1, D_MODEL), lambda b, i: (k, row0 // nbk + b, 0, 0))


def _ffn(x, mods, first, row0, g, wg, wu, wd, tm, nbk, merge=None):
    nb, s, _ = x.shape
    row = lambda w: pl.BlockSpec((nbk, tm, w), lambda b, i: (b, i, 0))
    weights = [_const_spec((1, 1, D_MODEL)), _const_spec(wg.shape), _const_spec(wu.shape), _const_spec(wd.shape)]
    args = [mods, mods, mods, g.reshape(1, 1, D_MODEL), wg, wu, wd]
    in_specs = [_mod_spec(first + k, nbk, row0) for k in range(3)] + weights
    if merge is not None:
        a, gas, mb, mgate, wpa, wo = merge
        args = [a, gas, mb, mods, wpa, wo] + args
        in_specs = [row(WIDTH_A), row(D_MODEL), row(D_MODEL), _mod_spec(mgate, nbk, row0), _const_spec(wpa.shape),
                    _const_spec(wo.shape)] + in_specs
    return pl.pallas_call(
        functools.partial(_ffn_kernel, merge=merge is not None),
        grid=(nb // nbk, s // tm),
        in_specs=[row(D_MODEL)] + in_specs,
        out_specs=row(D_MODEL),
        out_shape=jax.ShapeDtypeStruct(x.shape, F32),
        compiler_params=_params(("arbitrary", "arbitrary")),
        name="ffn_half_step_merged" if merge is not None else "ffn_half_step",
    )(x, *args)


def _mix_project(x_ref, shift_ref, scale_ref, g_ref, wcat_ref, z_ref):
    n = _flat((_rms(x_ref[...]) * g_ref[...]) * (1.0 + scale_ref[0]) + shift_ref[0])
    z_ref[...] = jnp.dot(n.astype(BF16), wcat_ref[...], preferred_element_type=F32)


def _mix_finish(z_ref, first_tile, bf_ref, gq_ref, gk_ref, gv_ref, wsp_ref, bsp_ref, wpb_ref, outs,
                *, tm, mix_len, mix_period, prompt):
    if prompt:
        qa_ref, ka_ref, va_ref, ko_ref, vo_ref, lf_ref, gas_ref, mb_ref, carry_ref = outs
    else:
        qo_ref, ko_ref, vo_ref, lf_ref, vbn_ref, gas_ref, mb_ref = outs
    o_q, o_k, o_v = 0, WIDTH_A, 2 * WIDTH_A
    o_zb = 3 * WIDTH_A
    o_ga = o_zb + 2 * WIDTH_B
    o_gb = o_ga + D_MODEL
    o_f = o_gb + D_MODEL

    lane = lax.broadcasted_iota(jnp.int32, (1, LANES), 1)
    logf = jnp.where(lane < N_HEADS, _log_sigmoid(z_ref[:, o_f:o_f + LANES] + bf_ref[...]), 0.0)
    if prompt:
        lf_ref[0] = logf.T[:N_HEADS, :]
    else:
        lf_ref[0] = logf

    if prompt:
        c = _cumsum_rows(logf) + jnp.where(first_tile, 0.0, carry_ref[...])
        carry_ref[...] = c[tm - 1:tm, :]
        pieces = _split3(c * LOG2E)

    for p in range(N_PAIRS):
        sl = slice(p * LANES, (p + 1) * LANES)
        qn = _head_norm(z_ref[:, o_q + p * LANES:o_q + (p + 1) * LANES], gq_ref[...])
        kn = _head_norm(z_ref[:, o_k + p * LANES:o_k + (p + 1) * LANES], gk_ref[...])
        v = z_ref[:, o_v + p * LANES:o_v + (p + 1) * LANES]
        ko_ref[0, :, sl] = kn
        vo_ref[0, :, sl] = v
        if prompt:
            he, ho = 2 * p, 2 * p + 1
            qe, qo = _head_tiles(qn * (ATTN_SCALE * LOG2E), _aug_q(pieces, he), _aug_q(pieces, ho))
            ke, ko = _head_tiles(kn, _aug_k(pieces, he), _aug_k(pieces, ho))
            ve, vo = _head_tiles(v, 1.0, 1.0)
            qa_ref[0, he] = qe.astype(BF16)
            qa_ref[0, ho] = qo.astype(BF16)
            ka_ref[0, he] = ke.astype(BF16)
            ka_ref[0, ho] = ko.astype(BF16)
            va_ref[0, he] = ve.astype(BF16)
            va_ref[0, ho] = vo.astype(BF16)
        else:
            qo_ref[0, :, sl] = qn

    zb = jax.nn.gelu(z_ref[:, o_zb:o_zb + 2 * WIDTH_B])
    u = zb[:, :WIDTH_B]
    vbn = _rms(zb[:, WIDTH_B:]) * gv_ref[...]
    if not prompt:
        vbn_ref[0] = vbn
    vbb = vbn.astype(BF16)
    keep = _tril_ones(mix_len, mix_period)
    wm = [jnp.where(keep, wsp_ref[g], 0.0).astype(BF16) for g in range(N_GROUPS)]
    rows = []
    for c0 in range(0, tm, mix_len):
        cols = [jnp.dot(wm[g], vbb[c0:c0 + mix_len, g * GROUP_DIM:(g + 1) * GROUP_DIM],
                        preferred_element_type=F32) for g in range(N_GROUPS)]
        rows.append(jnp.concatenate(cols, axis=1) + bsp_ref[...])
    mixed = rows[0] if len(rows) == 1 else jnp.concatenate(rows, axis=0)
    b = (u * mixed).astype(BF16)
    pb = jnp.dot(b, wpb_ref[...], preferred_element_type=F32)
    mb_ref[0] = (_sigmoid(z_ref[:, o_gb:o_gb + D_MODEL]) * pb).astype(BF16)
    gas_ref[0] = _sigmoid(z_ref[:, o_ga:o_ga + D_MODEL]).astype(BF16)


def _mix_kernel(x_ref, shift_ref, scale_ref, g_ref, wcat_ref, bf_ref, gq_ref, gk_ref, gv_ref,
                wsp_ref, bsp_ref, wpb_ref, *rest, n_tiles, **static):
    n_z = min(n_tiles, 2)
    z_refs = rest[-n_z:]
    outs = rest[:-n_z]
    project = functools.partial(_mix_project, x_ref, shift_ref, scale_ref, g_ref, wcat_ref)
    finish = functools.partial(_mix_finish, bf_ref=bf_ref, gq_ref=gq_ref, gk_ref=gk_ref, gv_ref=gv_ref,
                               wsp_ref=wsp_ref, bsp_ref=bsp_ref, wpb_ref=wpb_ref, outs=outs, **static)
    i = pl.program_id(1)
    if n_tiles == 1:
        project(z_refs[0])
        finish(z_refs[0], True)
        return

    @pl.when(i == 0)
    def _():
        project(z_refs[0])

    for parity in range(2):
        @pl.when((i > 0) & (i < n_tiles) & (i % 2 == parity))
        def _():
            project(z_refs[parity])
            finish(z_refs[1 - parity], i == 1)

    @pl.when(i == n_tiles)
    def _():
        finish(z_refs[(n_tiles - 1) % 2], n_tiles == 1)


def _mix(x, mods, first, row0, g, wcat, bf, gq, gk, gv, wsp, bsp, wpb, *, rows, nbk, mix_len, mix_period, prompt):
    nb, s, _ = x.shape
    tm = nbk * rows
    n_tiles = s // rows
    steps = n_tiles if n_tiles == 1 else n_tiles + 1
    done = lambda i: jnp.maximum(i - 1, 0) if n_tiles > 1 else i
    row = lambda w: pl.BlockSpec((1, tm, w), lambda b, i: (b, done(i), 0))
    head = pl.BlockSpec((1, N_HEADS, tm, LANES), lambda b, i: (b, 0, done(i), 0))
    x_spec = pl.BlockSpec((nbk, rows, D_MODEL), lambda b, i: (b, jnp.minimum(i, n_tiles - 1), 0))
    sds = jax.ShapeDtypeStruct
    tok = lambda w, dt: sds((nb // nbk, nbk * s, w), dt)
    hm = sds((nb // nbk, N_HEADS, nbk * s, LANES), BF16)
    if prompt:
        lf_spec = pl.BlockSpec((1, N_HEADS, tm), lambda b, i: (b, 0, done(i)))
        out_specs = [head, head, head, row(WIDTH_A), row(WIDTH_A), lf_spec, row(D_MODEL), row(D_MODEL)]
        out_shape = [hm, hm, hm, tok(WIDTH_A, F32), tok(WIDTH_A, F32), sds((nb, N_HEADS, s), F32),
                     tok(D_MODEL, BF16), tok(D_MODEL, BF16)]
        scratch = [pltpu.VMEM((1, LANES), F32)]
    else:
        out_specs = [row(WIDTH_A), row(WIDTH_A), row(WIDTH_A), row(LANES), row(WIDTH_B),
                     row(D_MODEL), row(D_MODEL)]
        out_shape = [tok(WIDTH_A, F32), tok(WIDTH_A, F32), tok(WIDTH_A, F32), tok(LANES, F32),
                     tok(WIDTH_B, F32), tok(D_MODEL, BF16), tok(D_MODEL, BF16)]
        scratch = []
    scratch += [pltpu.VMEM((tm, wcat.shape[1]), F32)] * min(n_tiles, 2)
    kern = functools.partial(_mix_kernel, n_tiles=n_tiles, tm=tm, mix_len=mix_len, mix_period=mix_period,
                             prompt=prompt)
    return pl.pallas_call(
        kern,
        grid=(nb // nbk, steps),
        in_specs=[x_spec, _mod_spec(first, nbk, row0), _mod_spec(first + 1, nbk, row0), _const_spec((1, 1, D_MODEL)),
                  _const_spec(wcat.shape), _const_spec(bf.shape), _const_spec(gq.shape),
                  _const_spec(gk.shape), _const_spec(gv.shape), _const_spec(wsp.shape),
                  _const_spec(bsp.shape), _const_spec(wpb.shape)],
        out_specs=out_specs,
        out_shape=out_shape,
        scratch_shapes=scratch,
        compiler_params=_params(("arbitrary", "arbitrary")),
        name="mixer_in_prompt" if prompt else "mixer_in_sample",
    )(x, mods, mods, g.reshape(1, 1, D_MODEL), wcat, bf, gq, gk, gv, wsp, bsp, wpb)


def _scores(q, k):
    return lax.dot_general(q, k, (((1,), (1,)), ((), ())), preferred_element_type=F32)


def _softmax_update(s, v, m_old, acc_old, mask):
    if mask is not None:
        s = jnp.where(mask, s, -jnp.inf)
    m_new = jnp.maximum(m_old, jnp.max(s, axis=-1, keepdims=True))
    cols = s.shape[1]
    m_wide = jnp.concatenate([m_new] * (cols // LANES), axis=1) if cols >= LANES else m_new[:, :cols]
    p = jnp.exp2(s - m_wide)
    alpha = jnp.exp2(m_old - m_new)
    acc = alpha * acc_old + jnp.dot(p.astype(BF16), v, preferred_element_type=F32)
    return m_new, acc


def _pair_output(acc_even, acc_odd):
    lane = lax.broadcasted_iota(jnp.int32, (1, LANES), 1)
    oe = acc_even * pltpu.roll(1.0 / acc_even, HEAD_DIM, 1)
    oo = acc_odd * pltpu.roll(1.0 / acc_odd, HEAD_DIM, 1)
    return jnp.where(lane < HEAD_DIM, oe, pltpu.roll(oo, HEAD_DIM, 1))


def _attn_kernel(q_ref, qn_ref, k_ref, v_ref, o_ref, s0_ref, s1_ref, acc_ref, m_ref, *, tq, tk):
    qi = pl.program_id(2)
    m_ref[...] = jnp.full(m_ref.shape, -jnp.inf, F32)
    acc_ref[...] = jnp.zeros(acc_ref.shape, F32)

    def keys(j):
        return pl.ds(pl.multiple_of(j * tk, tk), tk)

    def qk(src_ref, j, s_ref, r0=0):
        for h in range(2):
            s_ref[h, r0:, :] = _scores(src_ref[0, h, r0:, :], k_ref[0, h, keys(j), :])

    def soft_pv(j, s_ref, mask, r0=0):
        for h in range(2):
            m_new, acc = _softmax_update(s_ref[h, r0:, :], v_ref[0, h, keys(j), :],
                                         m_ref[h, r0:, :], acc_ref[h, r0:, :], mask)
            m_ref[h, r0:, :] = m_new
            acc_ref[h, r0:, :] = acc

    @pl.when(qi == 0)
    def _():
        qk(q_ref, 0, s0_ref)

    def body(u, carry):
        j = 2 * u
        qk(q_ref, j + 1, s1_ref)
        soft_pv(j, s0_ref, None)
        qk(q_ref, j + 2, s0_ref)
        soft_pv(j + 1, s1_ref, None)
        return carry

    def body_pair(w, carry):
        body(2 * w, carry)
        return body(2 * w + 1, carry)

    lax.fori_loop(0, qi // 2, body_pair, 0)

    @pl.when(qi % 2 == 1)
    def _():
        body(qi - 1, 0)

    row = lax.broadcasted_iota(jnp.int32, (tq, tk), 0)
    col = lax.broadcasted_iota(jnp.int32, (tq, tk), 1)
    qk(q_ref, 2 * qi + 1, s1_ref, tk)
    soft_pv(2 * qi, s0_ref, col <= row)
    qk(qn_ref, 0, s0_ref)
    soft_pv(2 * qi + 1, s1_ref, _tril_ones(tk), tk)

    o_ref[0] = _pair_output(acc_ref[0], acc_ref[1]).astype(o_ref.dtype)


def _attn(qa, ka, va, tk):
    nb, _, s, _ = qa.shape
    tq = 2 * tk
    nq = s // tq
    kv = pl.BlockSpec((1, 2, s, LANES), lambda b, p, i: (b, p, 0, 0))
    return pl.pallas_call(
        functools.partial(_attn_kernel, tq=tq, tk=tk),
        grid=(nb, N_PAIRS, nq),
        in_specs=[pl.BlockSpec((1, 2, tq, LANES), lambda b, p, i: (b, p, i, 0)),
                  pl.BlockSpec((1, 2, tq, LANES), lambda b, p, i: (b, p, jnp.minimum(i + 1, nq - 1), 0)),
                  kv, kv],
        out_specs=pl.BlockSpec((1, tq, LANES), lambda b, p, i: (b, i, p)),
        out_shape=jax.ShapeDtypeStruct((nb, s, WIDTH_A), BF16),
        scratch_shapes=[pltpu.VMEM((2, tq, tk), F32), pltpu.VMEM((2, tq, tk), F32),
                        pltpu.VMEM((2, tq, LANES), F32), pltpu.VMEM((2, tq, LANES), F32)],
        compiler_params=_params(("arbitrary", "arbitrary", "arbitrary"), VMEM_LIMIT_ATTN),
        name="fox_prompt_attention",
    )(qa, qa, ka, va)


def _cumsum_lanes(x):
    n = x.shape[1]
    lane = lax.broadcasted_iota(jnp.int32, x.shape, 1)
    k = 1
    while k < n:
        x = x + jnp.where(lane >= k, pltpu.roll(x, k, 1), 0.0)
        k *= 2
    return x


def _attn_sample_kernel(q_ref, k_ref, v_ref, lf_ref, kc_ref, vc_ref, lfc_ref, o_ref, *, t_new):
    lfc = lfc_ref[0]
    c_cache = _cumsum_lanes(lfc)
    p_len = lfc.shape[1]
    tail = (c_cache[:, p_len - 1:p_len] - c_cache) * LOG2E
    c_rel = _cumsum_rows(lf_ref[0][:, :N_HEADS])
    pieces = _split3(c_rel * LOG2E)
    causal = _tril_ones(t_new)
    lane = lax.broadcasted_iota(jnp.int32, (1, LANES), 1)
    lo = lane < HEAD_DIM
    widen = lambda x: jnp.pad(x, ((0, 0), (0, LANES - HEAD_DIM)))
    for p in range(N_PAIRS):
        sl = slice(p * LANES, (p + 1) * LANES)
        he, ho = 2 * p, 2 * p + 1
        q_t = _head_tiles(q_ref[0, :, sl] * (ATTN_SCALE * LOG2E), _aug_q(pieces, he), _aug_q(pieces, ho))
        kn_t = _head_tiles(k_ref[0, :, sl], _aug_k(pieces, he), _aug_k(pieces, ho))
        vn_t = _head_tiles(v_ref[0, :, sl], 1.0, 1.0)
        res = []
        for j, h in enumerate((he, ho)):
            q = q_t[j].astype(BF16)
            s_c = jnp.dot(q[:, :HEAD_DIM], kc_ref[0, h].astype(BF16), preferred_element_type=F32)
            s_c = s_c + (c_rel[:, h:h + 1] * LOG2E + tail[h:h + 1, :])
            s_n = jnp.where(causal, _scores(q, kn_t[j].astype(BF16)), -jnp.inf)
            m = jnp.maximum(jnp.max(s_c, axis=-1, keepdims=True), jnp.max(s_n, axis=-1, keepdims=True))
            p_c = jnp.exp2(s_c - m)
            p_n = jnp.exp2(s_n - m)
            o_c = lax.dot_general(p_c.astype(BF16), vc_ref[0, h].astype(BF16), (((1,), (1,)), ((), ())),
                                  preferred_element_type=F32)
            acc = jnp.dot(p_n.astype(BF16), vn_t[j].astype(BF16), preferred_element_type=F32)
            acc = acc + jnp.where(lo, widen(o_c), jnp.sum(p_c, axis=-1, keepdims=True))
            res.append(acc)
        o_ref[0, :, sl] = _pair_output(res[0], res[1]).astype(o_ref.dtype)


def _attn_sample(q, k, v, lf, kc_t, vc_t, lfc_t):
    nb, t_new, _ = q.shape
    p_len = kc_t.shape[-1]
    new = lambda w: pl.BlockSpec((1, t_new, w), lambda b: (b, 0, 0))
    cache = pl.BlockSpec((1, N_HEADS, HEAD_DIM, p_len), lambda b: (b, 0, 0, 0))
    return pl.pallas_call(
        functools.partial(_attn_sample_kernel, t_new=t_new),
        grid=(nb,),
        in_specs=[new(WIDTH_A), new(WIDTH_A), new(WIDTH_A), new(LANES), cache, cache,
                  pl.BlockSpec((1, N_HEADS, p_len), lambda b: (b, 0, 0))],
        out_specs=new(WIDTH_A),
        out_shape=jax.ShapeDtypeStruct((nb, t_new, WIDTH_A), BF16),
        compiler_params=_params(("arbitrary",)),
        name="fox_sample_attention",
    )(q, k, v, lf, kc_t, vc_t, lfc_t)


def _pad_lanes(v, fill=0.0):
    v = v.reshape(1, -1)
    return jnp.pad(v, ((0, 0), (0, LANES - v.shape[1])), constant_values=fill)


def _layer(xp, xs, mods, row0_s, cache_k, cache_v, cache_lf, g1, wg1, wu1, wd1, gm, w_in, b_forget, g_q, g_k, g_v,
           w_spatial, b_spatial, w_proj_a, w_proj_b, w_out, g2, wg2, wu2, wd2, *, tm, t_attn):
    nb_p, s_p, _ = xp.shape
    nb_s, t_new, _ = xs.shape
    rows_s = nb_s * t_new

    wg1, wu1, wd1 = wg1.astype(BF16), wu1.astype(BF16), wd1.astype(BF16)
    wg2, wu2, wd2 = wg2.astype(BF16), wu2.astype(BF16), wd2.astype(BF16)
    o_f = 3 * WIDTH_A
    o_zb = o_f + N_HEADS
    wf = jnp.pad(w_in[:, o_f:o_zb], ((0, 0), (0, LANES - N_HEADS)))
    wcat = jnp.concatenate([w_in[:, :o_f], w_in[:, o_zb:], wf], axis=1).astype(BF16)
    bf = _pad_lanes(b_forget)
    gq = jnp.tile(g_q, 2).reshape(1, LANES)
    gk = jnp.tile(g_k, 2).reshape(1, LANES)
    gv = g_v.reshape(1, WIDTH_B)
    bsp = jnp.repeat(b_spatial.T, GROUP_DIM, axis=1)
    wpa = w_proj_a.astype(BF16)
    wpb = w_proj_b.astype(BF16)
    wo = w_out.astype(BF16)

    xp = _ffn(xp, mods, 0, 0, g1, wg1, wu1, wd1, tm, 1)
    xs = _ffn(xs, mods, 0, row0_s, g1, wg1, wu1, wd1, t_new, nb_s)

    qa, ka, va, k_p, v_p, lf_p, gas, mb = _mix(
        xp, mods, 3, 0, gm, wcat, bf, gq, gk, gv, w_spatial, bsp, wpb,
        rows=tm, nbk=1, mix_len=GMLP_CHUNK, mix_period=GMLP_CHUNK, prompt=True)
    a = _attn(qa, ka, va, t_attn)

    wsp_s = jnp.tile(w_spatial[:, :t_new, :t_new], (1, nb_s, nb_s))
    bsp_s = jnp.tile(bsp[:t_new], (nb_s, 1))
    q_s, k_s, v_s, lf_s, vbn_s, gas_s, mb_s = _mix(
        xs, mods, 3, row0_s, gm, wcat, bf, gq, gk, gv, wsp_s, bsp_s, wpb,
        rows=t_new, nbk=nb_s, mix_len=rows_s, mix_period=t_new, prompt=False)
    per_req = lambda z: z.reshape(nb_s, t_new, z.shape[-1])
    a_s = _attn_sample(per_req(q_s), per_req(k_s), per_req(v_s), per_req(lf_s),
                       jnp.transpose(cache_k, (0, 2, 3, 1)), jnp.transpose(cache_v, (0, 2, 3, 1)),
                       jnp.transpose(cache_lf, (0, 2, 1)))

    xp = _ffn(xp, mods, 6, 0, g2, wg2, wu2, wd2, tm, 1, merge=(a, gas, mb, 5, wpa, wo))
    xs = _ffn(xs, mods, 6, row0_s, g2, wg2, wu2, wd2, t_new, nb_s,
              merge=(a_s, per_req(gas_s), per_req(mb_s), 5, wpa, wo))

    heads = lambda z, nb, s: z.reshape(nb, s, N_HEADS, HEAD_DIM)
    outs = (heads(k_p, nb_p, s_p), heads(v_p, nb_p, s_p), jnp.transpose(lf_p, (0, 2, 1)),
            heads(per_req(k_s), nb_s, t_new), heads(per_req(v_s), nb_s, t_new),
            per_req(lf_s)[:, :, :N_HEADS], per_req(vbn_s))
    return xp, xs, outs


def kernel(x_prompt, x_sample, c_prompt, c_sample, cache_fox_k, cache_fox_v, cache_fox_logf, w_ada, b_ada, g_norm_ffn1, w_ffn1_gate, w_ffn1_up, w_ffn1_down, g_norm_mix, w_in, b_forget, g_q, g_k, g_gmlp_v, w_spatial, b_spatial, w_proj_a, w_proj_b, w_out, g_norm_ffn2, w_ffn2_gate, w_ffn2_up, w_ffn2_down):
    depth = w_ada.shape[0]
    nb_p, s_p, _ = x_prompt.shape
    nb_s = x_sample.shape[0]
    tm = min(512, s_p)
    t_attn = min(512, s_p // 2)
    xp, xs = x_prompt, x_sample
    row0_s = -(-nb_p // nb_s) * nb_s
    c_all = jnp.concatenate([jnp.pad(c_prompt, ((0, row0_s - nb_p), (0, 0))), c_sample], axis=0)
    c_all = jnp.pad(c_all, ((0, -c_all.shape[0] % SUBLANES), (0, 0)))
    stacks = [[] for _ in range(7)]
    for l in range(depth):
        mods = _ada(c_all, w_ada[l], b_ada[l])[:, :, None, :]
        xp, xs, outs = _layer(
            xp, xs, mods, row0_s, cache_fox_k[l], cache_fox_v[l], cache_fox_logf[l],
            g_norm_ffn1[l], w_ffn1_gate[l], w_ffn1_up[l], w_ffn1_down[l],
            g_norm_mix[l], w_in[l], b_forget[l], g_q[l], g_k[l], g_gmlp_v[l], w_spatial[l], b_spatial[l],
            w_proj_a[l], w_proj_b[l], w_out[l],
            g_norm_ffn2[l], w_ffn2_gate[l], w_ffn2_up[l], w_ffn2_down[l], tm=tm, t_attn=t_attn)
        for st, o in zip(stacks, outs):
            st.append(o)
    return (xp, xs) + tuple(jnp.stack(st) for st in stacks)
```

```python
import functools

import jax
import jax.numpy as jnp
from jax import lax
from jax.experimental import pallas as pl
from jax.experimental.pallas import tpu as pltpu

D_MODEL = 1024
N_HEADS = 8
HEAD_DIM = 64
WIDTH_A = N_HEADS * HEAD_DIM
N_GROUPS = 4
GROUP_DIM = 128
WIDTH_B = N_GROUPS * GROUP_DIM
GMLP_CHUNK = 128
D_FF = 2816
N_MOD = 9
EPS = 1e-6
ATTN_SCALE = HEAD_DIM ** -0.5
LOG2E = 1.4426950408889634

LANES = 128
SUBLANES = 8
N_PAIRS = N_HEADS // 2
VMEM_LIMIT = 56 * 1024 * 1024

F32 = jnp.float32
BF16 = jnp.bfloat16


def _const_spec(shape):
    nd = len(shape)
    return pl.BlockSpec(shape, lambda *_: (0,) * nd, pipeline_mode=pl.Buffered(1))


def _params(sem, vmem_limit=VMEM_LIMIT):
    return pltpu.CompilerParams(dimension_semantics=sem, vmem_limit_bytes=vmem_limit)


def _rms(x):
    return x * lax.rsqrt(jnp.mean(x * x, axis=-1, keepdims=True) + EPS)


def _split3(c):
    hi = c.astype(BF16).astype(F32)
    r = c - hi
    mid = r.astype(BF16).astype(F32)
    lo = (r - mid).astype(BF16).astype(F32)
    return hi, mid, lo


def _tril_ones(n, period=None):
    row = lax.broadcasted_iota(jnp.int32, (n, n), 0)
    col = lax.broadcasted_iota(jnp.int32, (n, n), 1)
    keep = col <= row
    if period is not None and period != n:
        keep = keep & ((row // period) == (col // period))
    return keep


def _cumsum_rows(x):
    n, width = x.shape
    groups = n // SUBLANES
    x3 = x.reshape(groups, SUBLANES, width)
    sub = lax.broadcasted_iota(jnp.int32, (1, SUBLANES, width), 1)
    k = 1
    while k < SUBLANES:
        x3 = x3 + jnp.where(sub >= k, pltpu.roll(x3, k, 1), 0.0)
        k *= 2
    tot = x3[:, SUBLANES - 1:SUBLANES, :]
    k = 1
    while k < groups:
        tot = tot + jnp.concatenate([jnp.zeros((k, 1, width), F32), tot[:groups - k]], axis=0)
        k *= 2
    before = jnp.concatenate([jnp.zeros((1, 1, width), F32), tot[:groups - 1]], axis=0)
    return (x3 + before).reshape(n, width)


def _head_norm(slab, gain):
    lane = lax.broadcasted_iota(jnp.int32, (1, LANES), 1)
    lo = lane < HEAD_DIM
    sq = slab * slab
    ss_lo = jnp.sum(jnp.where(lo, sq, 0.0), axis=-1, keepdims=True)
    ss_hi = jnp.sum(jnp.where(lo, 0.0, sq), axis=-1, keepdims=True)
    r = jnp.where(lo, lax.rsqrt(ss_lo / HEAD_DIM + EPS), lax.rsqrt(ss_hi / HEAD_DIM + EPS))
    return slab * r * gain


def _head_tiles(slab, fill_even, fill_odd):
    lane = lax.broadcasted_iota(jnp.int32, (1, LANES), 1)
    lo = lane < HEAD_DIM
    even = jnp.where(lo, slab, fill_even)
    odd = jnp.where(lo, pltpu.roll(slab, HEAD_DIM, 1), fill_odd)
    return even, odd


def _aug_q(pieces, h):
    lane = lax.broadcasted_iota(jnp.int32, (1, LANES), 1)
    hi, mid, lo = (p[:, h:h + 1] for p in pieces)
    ones = jnp.where((lane >= HEAD_DIM + 3) & (lane < HEAD_DIM + 6), 1.0, 0.0)
    return jnp.where(lane == HEAD_DIM, hi, jnp.where(lane == HEAD_DIM + 1, mid,
                     jnp.where(lane == HEAD_DIM + 2, lo, ones)))


def _aug_k(pieces, h):
    lane = lax.broadcasted_iota(jnp.int32, (1, LANES), 1)
    hi, mid, lo = (-p[:, h:h + 1] for p in pieces)
    ones = jnp.where((lane >= HEAD_DIM) & (lane < HEAD_DIM + 3), 1.0, 0.0)
    return jnp.where(lane == HEAD_DIM + 3, hi, jnp.where(lane == HEAD_DIM + 4, mid,
                     jnp.where(lane == HEAD_DIM + 5, lo, ones)))


def _sigmoid(x):
    return 0.5 * jnp.tanh(0.5 * x) + 0.5


def _log_sigmoid(x):
    return jnp.minimum(x, 0.0) - jnp.log1p(jnp.exp(-jnp.abs(x)))


def _ada_kernel(c_ref, w_ref, b_ref, o_ref):
    h = jax.nn.silu(c_ref[...]).astype(BF16)
    o_ref[0] = jnp.dot(h, w_ref[...].astype(BF16), preferred_element_type=F32) + b_ref[...]


def _ada(c, w_ada, b_ada):
    rows = c.shape[0]
    n = w_ada.shape[1]
    tn = D_MODEL
    return pl.pallas_call(
        _ada_kernel,
        grid=(n // tn,),
        in_specs=[pl.BlockSpec((rows, D_MODEL), lambda j: (0, 0)),
                  pl.BlockSpec((D_MODEL, tn), lambda j: (0, j)),
                  pl.BlockSpec((1, tn), lambda j: (0, j))],
        out_specs=pl.BlockSpec((1, rows, tn), lambda j: (j, 0, 0)),
        out_shape=jax.ShapeDtypeStruct((n // tn, rows, tn), F32),
        compiler_params=_params(("arbitrary",)),
        name="ada_modulation",
    )(c, w_ada, b_ada.reshape(1, n))


def _flat(x3):
    return x3.reshape(x3.shape[0] * x3.shape[1], x3.shape[2])


def _merge_rows(a_ref, gas_ref, mb_ref, wpa_ref, wo_ref):
    pa = jnp.dot(_flat(a_ref[...]), wpa_ref[...], preferred_element_type=F32)
    m = _flat(gas_ref[...]).astype(F32) * pa + _flat(mb_ref[...]).astype(F32)
    return jnp.dot(m.astype(BF16), wo_ref[...], preferred_element_type=F32)


def _ffn_kernel(*refs, merge):
    if merge:
        (x_ref, a_ref, gas_ref, mb_ref, mgate_ref, wpa_ref, wo_ref,
         shift_ref, scale_ref, gate_ref, g_ref, wg_ref, wu_ref, wd_ref, o_ref) = refs
        x3 = x_ref[...]
        x3 = x3 + mgate_ref[0] * _merge_rows(a_ref, gas_ref, mb_ref, wpa_ref, wo_ref).reshape(x3.shape)
    else:
        x_ref, shift_ref, scale_ref, gate_ref, g_ref, wg_ref, wu_ref, wd_ref, o_ref = refs
        x3 = x_ref[...]
    h = _flat((_rms(x3) * g_ref[...]) * (1.0 + scale_ref[0]) + shift_ref[0]).astype(BF16)
    a = jax.nn.silu(jnp.dot(h, wg_ref[...], preferred_element_type=F32))
    a = a * jnp.dot(h, wu_ref[...], preferred_element_type=F32)
    y = jnp.dot(a.astype(BF16), wd_ref[...], preferred_element_type=F32)
    o_ref[...] = x3 + 0.5 * gate_ref[0] * y.reshape(x3.shape)


def _mod_spec(k, nbk, row0):
    return pl.BlockSpec((1, nbk, 1, D_MODEL), lambda b, i: (k, row0 // nbk + b, 0, 0))


def _ffn(x, mods, first, row0, g, wg, wu, wd, tm, nbk, merge=None):
    nb, s, _ = x.shape
    row = lambda w: pl.BlockSpec((nbk, tm, w), lambda b, i: (b, i, 0))
    weights = [_const_spec((1, 1, D_MODEL)), _const_spec(wg.shape), _const_spec(wu.shape), _const_spec(wd.shape)]
    args = [mods, mods, mods, g.reshape(1, 1, D_MODEL), wg, wu, wd]
    in_specs = [_mod_spec(first + k, nbk, row0) for k in range(3)] + weights
    if merge is not None:
        a, gas, mb, mgate, wpa, wo = merge
        args = [a, gas, mb, mods, wpa, wo] + args
        in_specs = [row(WIDTH_A), row(D_MODEL), row(D_MODEL), _mod_spec(mgate, nbk, row0), _const_spec(wpa.shape),
                    _const_spec(wo.shape)] + in_specs
    return pl.pallas_call(
        functools.partial(_ffn_kernel, merge=merge is not None),
        grid=(nb // nbk, s // tm),
        in_specs=[row(D_MODEL)] + in_specs,
        out_specs=row(D_MODEL),
        out_shape=jax.ShapeDtypeStruct(x.shape, F32),
        compiler_params=_params(("arbitrary", "arbitrary")),
        name="ffn_half_step_merged" if merge is not None else "ffn_half_step",
    )(x, *args)


def _mix_project(x_ref, shift_ref, scale_ref, g_ref, wcat_ref, z_ref):
    n = _flat((_rms(x_ref[...]) * g_ref[...]) * (1.0 + scale_ref[0]) + shift_ref[0])
    z_ref[...] = jnp.dot(n.astype(BF16), wcat_ref[...], preferred_element_type=F32)


def _mix_finish(z_ref, first_tile, bf_ref, gq_ref, gk_ref, gv_ref, wsp_ref, bsp_ref, wpb_ref, outs,
                *, tm, mix_len, mix_period, prompt):
    if prompt:
        qa_ref, ka_ref, va_ref, ko_ref, vo_ref, lf_ref, gas_ref, mb_ref, carry_ref = outs
    else:
        qo_ref, ko_ref, vo_ref, lf_ref, vbn_ref, gas_ref, mb_ref = outs
    o_q, o_k, o_v = 0, WIDTH_A, 2 * WIDTH_A
    o_zb = 3 * WIDTH_A
    o_ga = o_zb + 2 * WIDTH_B
    o_gb = o_ga + D_MODEL
    o_f = o_gb + D_MODEL

    lane = lax.broadcasted_iota(jnp.int32, (1, LANES), 1)
    logf = jnp.where(lane < N_HEADS, _log_sigmoid(z_ref[:, o_f:o_f + LANES] + bf_ref[...]), 0.0)
    if prompt:
        lf_ref[0] = logf.T[:N_HEADS, :]
    else:
        lf_ref[0] = logf

    if prompt:
        c = _cumsum_rows(logf) + jnp.where(first_tile, 0.0, carry_ref[...])
        carry_ref[...] = c[tm - 1:tm, :]
        pieces = _split3(c * LOG2E)

    for p in range(N_PAIRS):
        sl = slice(p * LANES, (p + 1) * LANES)
        qn = _head_norm(z_ref[:, o_q + p * LANES:o_q + (p + 1) * LANES], gq_ref[...])
        kn = _head_norm(z_ref[:, o_k + p * LANES:o_k + (p + 1) * LANES], gk_ref[...])
        v = z_ref[:, o_v + p * LANES:o_v + (p + 1) * LANES]
        ko_ref[0, :, sl] = kn
        vo_ref[0, :, sl] = v
        if prompt:
            he, ho = 2 * p, 2 * p + 1
            qe, qo = _head_tiles(qn * (ATTN_SCALE * LOG2E), _aug_q(pieces, he), _aug_q(pieces, ho))
            ke, ko = _head_tiles(kn, _aug_k(pieces, he), _aug_k(pieces, ho))
            ve, vo = _head_tiles(v, 1.0, 1.0)
            qa_ref[0, he] = qe.astype(BF16)
            qa_ref[0, ho] = qo.astype(BF16)
            ka_ref[0, he] = ke.astype(BF16)
            ka_ref[0, ho] = ko.astype(BF16)
            va_ref[0, he] = ve.astype(BF16)
            va_ref[0, ho] = vo.astype(BF16)
        else:
            qo_ref[0, :, sl] = qn

    zb = jax.nn.gelu(z_ref[:, o_zb:o_zb + 2 * WIDTH_B])
    u = zb[:, :WIDTH_B]
    vbn = _rms(zb[:, WIDTH_B:]) * gv_ref[...]
    if not prompt:
        vbn_ref[0] = vbn
    vbb = vbn.astype(BF16)
    keep = _tril_ones(mix_len, mix_period)
    wm = [jnp.where(keep, wsp_ref[g], 0.0).astype(BF16) for g in range(N_GROUPS)]
    rows = []
    for c0 in range(0, tm, mix_len):
        cols = [jnp.dot(wm[g], vbb[c0:c0 + mix_len, g * GROUP_DIM:(g + 1) * GROUP_DIM],
                        preferred_element_type=F32) for g in range(N_GROUPS)]
        rows.append(jnp.concatenate(cols, axis=1) + bsp_ref[...])
    mixed = rows[0] if len(rows) == 1 else jnp.concatenate(rows, axis=0)
    b = (u * mixed).astype(BF16)
    pb = jnp.dot(b, wpb_ref[...], preferred_element_type=F32)
    mb_ref[0] = (_sigmoid(z_ref[:, o_gb:o_gb + D_MODEL]) * pb).astype(BF16)
    gas_ref[0] = _sigmoid(z_ref[:, o_ga:o_ga + D_MODEL]).astype(BF16)


def _mix_kernel(x_ref, shift_ref, scale_ref, g_ref, wcat_ref, bf_ref, gq_ref, gk_ref, gv_ref,
                wsp_ref, bsp_ref, wpb_ref, *rest, n_tiles, **static):
    n_z = min(n_tiles, 2)
    z_refs = rest[-n_z:]
    outs = rest[:-n_z]
    project = functools.partial(_mix_project, x_ref, shift_ref, scale_ref, g_ref, wcat_ref)
    finish = functools.partial(_mix_finish, bf_ref=bf_ref, gq_ref=gq_ref, gk_ref=gk_ref, gv_ref=gv_ref,
                               wsp_ref=wsp_ref, bsp_ref=bsp_ref, wpb_ref=wpb_ref, outs=outs, **static)
    i = pl.program_id(1)
    if n_tiles == 1:
        project(z_refs[0])
        finish(z_refs[0], True)
        return

    @pl.when(i == 0)
    def _():
        project(z_refs[0])

    for parity in range(2):
        @pl.when((i > 0) & (i < n_tiles) & (i % 2 == parity))
        def _():
            project(z_refs[parity])
            finish(z_refs[1 - parity], i == 1)

    @pl.when(i == n_tiles)
    def _():
        finish(z_refs[(n_tiles - 1) % 2], n_tiles == 1)


def _mix(x, mods, first, row0, g, wcat, bf, gq, gk, gv, wsp, bsp, wpb, *, rows, nbk, mix_len, mix_period, prompt):
    nb, s, _ = x.shape
    tm = nbk * rows
    n_tiles = s // rows
    steps = n_tiles if n_tiles == 1 else n_tiles + 1
    done = lambda i: jnp.maximum(i - 1, 0) if n_tiles > 1 else i
    row = lambda w: pl.BlockSpec((1, tm, w), lambda b, i: (b, done(i), 0))
    head = pl.BlockSpec((1, N_HEADS, tm, LANES), lambda b, i: (b, 0, done(i), 0))
    x_spec = pl.BlockSpec((nbk, rows, D_MODEL), lambda b, i: (b, jnp.minimum(i, n_tiles - 1), 0))
    sds = jax.ShapeDtypeStruct
    tok = lambda w, dt: sds((nb // nbk, nbk * s, w), dt)
    hm = sds((nb // nbk, N_HEADS, nbk * s, LANES), BF16)
    if prompt:
        lf_spec = pl.BlockSpec((1, N_HEADS, tm), lambda b, i: (b, 0, done(i)))
        out_specs = [head, head, head, row(WIDTH_A), row(WIDTH_A), lf_spec, row(D_MODEL), row(D_MODEL)]
        out_shape = [hm, hm, hm, tok(WIDTH_A, F32), tok(WIDTH_A, F32), sds((nb, N_HEADS, s), F32),
                     tok(D_MODEL, BF16), tok(D_MODEL, BF16)]
        scratch = [pltpu.VMEM((1, LANES), F32)]
    else:
        out_specs = [row(WIDTH_A), row(WIDTH_A), row(WIDTH_A), row(LANES), row(WIDTH_B),
                     row(D_MODEL), row(D_MODEL)]
        out_shape = [tok(WIDTH_A, F32), tok(WIDTH_A, F32), tok(WIDTH_A, F32), tok(LANES, F32),
                     tok(WIDTH_B, F32), tok(D_MODEL, BF16), tok(D_MODEL, BF16)]
        scratch = []
    scratch += [pltpu.VMEM((tm, wcat.shape[1]), F32)] * min(n_tiles, 2)
    kern = functools.partial(_mix_kernel, n_tiles=n_tiles, tm=tm, mix_len=mix_len, mix_period=mix_period,
                             prompt=prompt)
    return pl.pallas_call(
        kern,
        grid=(nb // nbk, steps),
        in_specs=[x_spec, _mod_spec(first, nbk, row0), _mod_spec(first + 1, nbk, row0), _const_spec((1, 1, D_MODEL)),
                  _const_spec(wcat.shape), _const_spec(bf.shape), _const_spec(gq.shape),
                  _const_spec(gk.shape), _const_spec(gv.shape), _const_spec(wsp.shape),
                  _const_spec(bsp.shape), _const_spec(wpb.shape)],
        out_specs=out_specs,
        out_shape=out_shape,
        scratch_shapes=scratch,
        compiler_params=_params(("arbitrary", "arbitrary")),
        name="mixer_in_prompt" if prompt else "mixer_in_sample",
    )(x, mods, mods, g.reshape(1, 1, D_MODEL), wcat, bf, gq, gk, gv, wsp, bsp, wpb)


def _scores(q, k):
    return lax.dot_general(q, k, (((1,), (1,)), ((), ())), preferred_element_type=F32)


def _softmax_update(s, v, m_old, acc_old, mask):
    if mask is not None:
        s = jnp.where(mask, s, -jnp.inf)
    m_new = jnp.maximum(m_old, jnp.max(s, axis=-1, keepdims=True))
    cols = s.shape[1]
    m_wide = jnp.concatenate([m_new] * (cols // LANES), axis=1) if cols >= LANES else m_new[:, :cols]
    p = jnp.exp2(s - m_wide)
    alpha = jnp.exp2(m_old - m_new)
    acc = alpha * acc_old + jnp.dot(p.astype(BF16), v, preferred_element_type=F32)
    return m_new, acc


def _pair_output(acc_even, acc_odd):
    lane = lax.broadcasted_iota(jnp.int32, (1, LANES), 1)
    oe = acc_even * pltpu.roll(1.0 / acc_even, HEAD_DIM, 1)
    oo = acc_odd * pltpu.roll(1.0 / acc_odd, HEAD_DIM, 1)
    return jnp.where(lane < HEAD_DIM, oe, pltpu.roll(oo, HEAD_DIM, 1))


def _attn_kernel(q_ref, qn_ref, kin_ref, vin_ref, o_ref, k_ref, v_ref, s0_ref, s1_ref, acc_ref, m_ref, *, tq, tk):
    qi = pl.program_id(2)
    m_ref[...] = jnp.full(m_ref.shape, -jnp.inf, F32)
    acc_ref[...] = jnp.zeros(acc_ref.shape, F32)
    own = pl.ds(pl.multiple_of(qi * tq, tq), tq)
    k_ref[:, own, :] = kin_ref[0]
    v_ref[:, own, :] = vin_ref[0]

    def keys(j):
        return pl.ds(pl.multiple_of(j * tk, tk), tk)

    def qk(src_ref, j, s_ref, r0=0):
        for h in range(2):
            s_ref[h, r0:, :] = _scores(src_ref[0, h, r0:, :], k_ref[h, keys(j), :])

    def soft_pv(j, s_ref, mask, r0=0):
        for h in range(2):
            m_new, acc = _softmax_update(s_ref[h, r0:, :], v_ref[h, keys(j), :],
                                         m_ref[h, r0:, :], acc_ref[h, r0:, :], mask)
            m_ref[h, r0:, :] = m_new
            acc_ref[h, r0:, :] = acc

    @pl.when(qi == 0)
    def _():
        qk(q_ref, 0, s0_ref)

    def body(u, carry):
        j = 2 * u
        qk(q_ref, j + 1, s1_ref)
        soft_pv(j, s0_ref, None)
        qk(q_ref, j + 2, s0_ref)
        soft_pv(j + 1, s1_ref, None)
        return carry

    def body_quad(w, carry):
        for r in range(4):
            body(4 * w + r, carry)
        return carry

    lax.fori_loop(0, qi // 4, body_quad, 0)

    @pl.when(qi % 4 >= 2)
    def _():
        first = 4 * (qi // 4)
        body(first, 0)
        body(first + 1, 0)

    def diagonal():
        row = lax.broadcasted_iota(jnp.int32, (tq, tk), 0)
        col = lax.broadcasted_iota(jnp.int32, (tq, tk), 1)
        qk(q_ref, 2 * qi + 1, s1_ref, tk)
        soft_pv(2 * qi, s0_ref, col <= row)
        qk(qn_ref, 0, s0_ref)
        soft_pv(2 * qi + 1, s1_ref, _tril_ones(tk), tk)

    @pl.when(qi % 2 == 1)
    def _():
        body(qi - 1, 0)
        diagonal()

    @pl.when(qi % 2 == 0)
    def _():
        diagonal()

    o_ref[0] = _pair_output(acc_ref[0], acc_ref[1]).astype(o_ref.dtype)


def _attn(qa, ka, va, tk):
    nb, _, s, _ = qa.shape
    tq = 2 * tk
    nq = s // tq
    kv = pl.BlockSpec((1, 2, tq, LANES), lambda b, p, i: (b, p, i, 0))
    return pl.pallas_call(
        functools.partial(_attn_kernel, tq=tq, tk=tk),
        grid=(nb, N_PAIRS, nq),
        in_specs=[pl.BlockSpec((1, 2, tq, LANES), lambda b, p, i: (b, p, i, 0)),
                  pl.BlockSpec((1, 2, tq, LANES), lambda b, p, i: (b, p, jnp.minimum(i + 1, nq - 1), 0)),
                  kv, kv],
        out_specs=pl.BlockSpec((1, tq, LANES), lambda b, p, i: (b, i, p)),
        out_shape=jax.ShapeDtypeStruct((nb, s, WIDTH_A), BF16),
        scratch_shapes=[pltpu.VMEM((2, s, LANES), BF16), pltpu.VMEM((2, s, LANES), BF16),
                        pltpu.VMEM((2, tq, tk), F32), pltpu.VMEM((2, tq, tk), F32),
                        pltpu.VMEM((2, tq, LANES), F32), pltpu.VMEM((2, tq, LANES), F32)],
        compiler_params=_params(("arbitrary", "arbitrary", "arbitrary")),
        name="fox_prompt_attention",
    )(qa, qa, ka, va)


def _cumsum_lanes(x):
    n = x.shape[1]
    lane = lax.broadcasted_iota(jnp.int32, x.shape, 1)
    k = 1
    while k < n:
        x = x + jnp.where(lane >= k, pltpu.roll(x, k, 1), 0.0)
        k *= 2
    return x


def _attn_sample_kernel(q_ref, k_ref, v_ref, lf_ref, kc_ref, vc_ref, lfc_ref, o_ref, *, t_new):
    lfc = lfc_ref[0]
    c_cache = _cumsum_lanes(lfc)
    p_len = lfc.shape[1]
    tail = (c_cache[:, p_len - 1:p_len] - c_cache) * LOG2E
    c_rel = _cumsum_rows(lf_ref[0][:, :N_HEADS])
    pieces = _split3(c_rel * LOG2E)
    causal = _tril_ones(t_new)
    lane = lax.broadcasted_iota(jnp.int32, (1, LANES), 1)
    lo = lane < HEAD_DIM
    widen = lambda x: jnp.pad(x, ((0, 0), (0, LANES - HEAD_DIM)))
    for p in range(N_PAIRS):
        sl = slice(p * LANES, (p + 1) * LANES)
        he, ho = 2 * p, 2 * p + 1
        q_t = _head_tiles(q_ref[0, :, sl] * (ATTN_SCALE * LOG2E), _aug_q(pieces, he), _aug_q(pieces, ho))
        kn_t = _head_tiles(k_ref[0, :, sl], _aug_k(pieces, he), _aug_k(pieces, ho))
        vn_t = _head_tiles(v_ref[0, :, sl], 1.0, 1.0)
        res = []
        for j, h in enumerate((he, ho)):
            q = q_t[j].astype(BF16)
            s_c = jnp.dot(q[:, :HEAD_DIM], kc_ref[0, h].astype(BF16), preferred_element_type=F32)
            s_c = s_c + (c_rel[:, h:h + 1] * LOG2E + tail[h:h + 1, :])
            s_n = jnp.where(causal, _scores(q, kn_t[j].astype(BF16)), -jnp.inf)
            m = jnp.maximum(jnp.max(s_c, axis=-1, keepdims=True), jnp.max(s_n, axis=-1, keepdims=True))
            p_c = jnp.exp2(s_c - m)
            p_n = jnp.exp2(s_n - m)
            o_c = lax.dot_general(p_c.astype(BF16), vc_ref[0, h].astype(BF16), (((1,), (1,)), ((), ())),
                                  preferred_element_type=F32)
            acc = jnp.dot(p_n.astype(BF16), vn_t[j].astype(BF16), preferred_element_type=F32)
            acc = acc + jnp.where(lo, widen(o_c), jnp.sum(p_c, axis=-1, keepdims=True))
            res.append(acc)
        o_ref[0, :, sl] = _pair_output(res[0], res[1]).astype(o_ref.dtype)


def _attn_sample(q, k, v, lf, kc_t, vc_t, lfc_t):
    nb, t_new, _ = q.shape
    p_len = kc_t.shape[-1]
    new = lambda w: pl.BlockSpec((1, t_new, w), lambda b: (b, 0, 0))
    cache = pl.BlockSpec((1, N_HEADS, HEAD_DIM, p_len), lambda b: (b, 0, 0, 0))
    return pl.pallas_call(
        functools.partial(_attn_sample_kernel, t_new=t_new),
        grid=(nb,),
        in_specs=[new(WIDTH_A), new(WIDTH_A), new(WIDTH_A), new(LANES), cache, cache,
                  pl.BlockSpec((1, N_HEADS, p_len), lambda b: (b, 0, 0))],
        out_specs=new(WIDTH_A),
        out_shape=jax.ShapeDtypeStruct((nb, t_new, WIDTH_A), BF16),
        compiler_params=_params(("arbitrary",)),
        name="fox_sample_attention",
    )(q, k, v, lf, kc_t, vc_t, lfc_t)


def _pad_lanes(v, fill=0.0):
    v = v.reshape(1, -1)
    return jnp.pad(v, ((0, 0), (0, LANES - v.shape[1])), constant_values=fill)


def _layer(xp, xs, mods, row0_s, cache_k, cache_v, cache_lf, g1, wg1, wu1, wd1, gm, w_in, b_forget, g_q, g_k, g_v,
           w_spatial, b_spatial, w_proj_a, w_proj_b, w_out, g2, wg2, wu2, wd2, *, tm, t_attn):
    nb_p, s_p, _ = xp.shape
    nb_s, t_new, _ = xs.shape
    rows_s = nb_s * t_new

    wg1, wu1, wd1 = wg1.astype(BF16), wu1.astype(BF16), wd1.astype(BF16)
    wg2, wu2, wd2 = wg2.astype(BF16), wu2.astype(BF16), wd2.astype(BF16)
    o_f = 3 * WIDTH_A
    o_zb = o_f + N_HEADS
    wf = jnp.pad(w_in[:, o_f:o_zb], ((0, 0), (0, LANES - N_HEADS)))
    wcat = jnp.concatenate([w_in[:, :o_f], w_in[:, o_zb:], wf], axis=1).astype(BF16)
    bf = _pad_lanes(b_forget)
    gq = jnp.tile(g_q, 2).reshape(1, LANES)
    gk = jnp.tile(g_k, 2).reshape(1, LANES)
    gv = g_v.reshape(1, WIDTH_B)
    bsp = jnp.repeat(b_spatial.T, GROUP_DIM, axis=1)
    wpa = w_proj_a.astype(BF16)
    wpb = w_proj_b.astype(BF16)
    wo = w_out.astype(BF16)

    xs = _ffn(xs, mods, 0, row0_s, g1, wg1, wu1, wd1, t_new, nb_s)
    wsp_s = jnp.tile(w_spatial[:, :t_new, :t_new], (1, nb_s, nb_s))
    bsp_s = jnp.tile(bsp[:t_new], (nb_s, 1))
    q_s, k_s, v_s, lf_s, vbn_s, gas_s, mb_s = _mix(
        xs, mods, 3, row0_s, gm, wcat, bf, gq, gk, gv, wsp_s, bsp_s, wpb,
        rows=t_new, nbk=nb_s, mix_len=rows_s, mix_period=t_new, prompt=False)
    per_req = lambda z: z.reshape(nb_s, t_new, z.shape[-1])
    a_s = _attn_sample(per_req(q_s), per_req(k_s), per_req(v_s), per_req(lf_s),
                       jnp.transpose(cache_k, (0, 2, 3, 1)), jnp.transpose(cache_v, (0, 2, 3, 1)),
                       jnp.transpose(cache_lf, (0, 2, 1)))
    xs = _ffn(xs, mods, 6, row0_s, g2, wg2, wu2, wd2, t_new, nb_s,
              merge=(a_s, per_req(gas_s), per_req(mb_s), 5, wpa, wo))

    xp = _ffn(xp, mods, 0, 0, g1, wg1, wu1, wd1, tm, 1)
    qa, ka, va, k_p, v_p, lf_p, gas, mb = _mix(
        xp, mods, 3, 0, gm, wcat, bf, gq, gk, gv, w_spatial, bsp, wpb,
        rows=tm, nbk=1, mix_len=GMLP_CHUNK, mix_period=GMLP_CHUNK, prompt=True)
    a = _attn(qa, ka, va, t_attn)
    xp = _ffn(xp, mods, 6, 0, g2, wg2, wu2, wd2, tm, 1, merge=(a, gas, mb, 5, wpa, wo))

    heads = lambda z, nb, s: z.reshape(nb, s, N_HEADS, HEAD_DIM)
    outs = (heads(k_p, nb_p, s_p), heads(v_p, nb_p, s_p), jnp.transpose(lf_p, (0, 2, 1)),
            heads(per_req(k_s), nb_s, t_new), heads(per_req(v_s), nb_s, t_new),
            per_req(lf_s)[:, :, :N_HEADS], per_req(vbn_s))
    return xp, xs, outs


def kernel(x_prompt, x_sample, c_prompt, c_sample, cache_fox_k, cache_fox_v, cache_fox_logf, w_ada, b_ada, g_norm_ffn1, w_ffn1_gate, w_ffn1_up, w_ffn1_down, g_norm_mix, w_in, b_forget, g_q, g_k, g_gmlp_v, w_spatial, b_spatial, w_proj_a, w_proj_b, w_out, g_norm_ffn2, w_ffn2_gate, w_ffn2_up, w_ffn2_down):
    depth = w_ada.shape[0]
    nb_p, s_p, _ = x_prompt.shape
    nb_s = x_sample.shape[0]
    tm = min(512, s_p)
    t_attn = min(512, s_p // 2)
    xp, xs = x_prompt, x_sample
    row0_s = -(-nb_p // nb_s) * nb_s
    c_all = jnp.concatenate([jnp.pad(c_prompt, ((0, row0_s - nb_p), (0, 0))), c_sample], axis=0)
    c_all = jnp.pad(c_all, ((0, -c_all.shape[0] % SUBLANES), (0, 0)))
    stacks = [[] for _ in range(7)]
    for l in range(depth):
        mods = _ada(c_all, w_ada[l], b_ada[l])[:, :, None, :]
        xp, xs, outs = _layer(
            xp, xs, mods, row0_s, cache_fox_k[l], cache_fox_v[l], cache_fox_logf[l],
            g_norm_ffn1[l], w_ffn1_gate[l], w_ffn1_up[l], w_ffn1_down[l],
            g_norm_mix[l], w_in[l], b_forget[l], g_q[l], g_k[l], g_gmlp_v[l], w_spatial[l], b_spatial[l],
            w_proj_a[l], w_proj_b[l], w_out[l],
            g_norm_ffn2[l], w_ffn2_gate[l], w_ffn2_up[l], w_ffn2_down[l], tm=tm, t_attn=t_attn)
        for st, o in zip(stacks, outs):
            st.append(o)
    return (xp, xs) + tuple(jnp.stack(st) for st in stacks)
```

```python
import functools

import jax
import jax.numpy as jnp
from jax import lax
from jax.experimental import pallas as pl
from jax.experimental.pallas import tpu as pltpu

D_MODEL = 1024
N_HEADS = 8
HEAD_DIM = 64
WIDTH_A = N_HEADS * HEAD_DIM
N_GROUPS = 4
GROUP_DIM = 128
WIDTH_B = N_GROUPS * GROUP_DIM
GMLP_CHUNK = 128
D_FF = 2816
FF_CHUNK = 256
N_MOD = 9
EPS = 1e-6
ATTN_SCALE = HEAD_DIM ** -0.5
LOG2E = 1.4426950408889634

LANES = 128
SUBLANES = 8
N_PAIRS = N_HEADS // 2
VMEM_LIMIT = 56 * 1024 * 1024

F32 = jnp.float32
BF16 = jnp.bfloat16


def _const_spec(shape):
    nd = len(shape)
    return pl.BlockSpec(shape, lambda *_: (0,) * nd, pipeline_mode=pl.Buffered(1))


def _params(sem, vmem_limit=VMEM_LIMIT):
    return pltpu.CompilerParams(dimension_semantics=sem, vmem_limit_bytes=vmem_limit)


def _rms(x):
    return x * lax.rsqrt(jnp.mean(x * x, axis=-1, keepdims=True) + EPS)


def _split3(c):
    hi = c.astype(BF16).astype(F32)
    r = c - hi
    mid = r.astype(BF16).astype(F32)
    lo = (r - mid).astype(BF16).astype(F32)
    return hi, mid, lo


def _tril_ones(n, period=None):
    row = lax.broadcasted_iota(jnp.int32, (n, n), 0)
    col = lax.broadcasted_iota(jnp.int32, (n, n), 1)
    keep = col <= row
    if period is not None and period != n:
        keep = keep & ((row // period) == (col // period))
    return keep


def _cumsum_rows(x):
    n, width = x.shape
    groups = n // SUBLANES
    x3 = x.reshape(groups, SUBLANES, width)
    sub = lax.broadcasted_iota(jnp.int32, (1, SUBLANES, width), 1)
    k = 1
    while k < SUBLANES:
        x3 = x3 + jnp.where(sub >= k, pltpu.roll(x3, k, 1), 0.0)
        k *= 2
    tot = x3[:, SUBLANES - 1:SUBLANES, :]
    k = 1
    while k < groups:
        tot = tot + jnp.concatenate([jnp.zeros((k, 1, width), F32), tot[:groups - k]], axis=0)
        k *= 2
    before = jnp.concatenate([jnp.zeros((1, 1, width), F32), tot[:groups - 1]], axis=0)
    return (x3 + before).reshape(n, width)


def _head_norm(slab, gain):
    lane = lax.broadcasted_iota(jnp.int32, (1, LANES), 1)
    lo = lane < HEAD_DIM
    sq = slab * slab
    ss_lo = jnp.sum(jnp.where(lo, sq, 0.0), axis=-1, keepdims=True)
    ss_hi = jnp.sum(jnp.where(lo, 0.0, sq), axis=-1, keepdims=True)
    r = jnp.where(lo, lax.rsqrt(ss_lo / HEAD_DIM + EPS), lax.rsqrt(ss_hi / HEAD_DIM + EPS))
    return slab * r * gain


def _head_tiles(slab, fill_even, fill_odd):
    lane = lax.broadcasted_iota(jnp.int32, (1, LANES), 1)
    lo = lane < HEAD_DIM
    even = jnp.where(lo, slab, fill_even)
    odd = jnp.where(lo, pltpu.roll(slab, HEAD_DIM, 1), fill_odd)
    return even, odd


def _aug_q(pieces, h):
    lane = lax.broadcasted_iota(jnp.int32, (1, LANES), 1)
    hi, mid, lo = (p[:, h:h + 1] for p in pieces)
    ones = jnp.where((lane >= HEAD_DIM + 3) & (lane < HEAD_DIM + 6), 1.0, 0.0)
    return jnp.where(lane == HEAD_DIM, hi, jnp.where(lane == HEAD_DIM + 1, mid,
                     jnp.where(lane == HEAD_DIM + 2, lo, ones)))


def _aug_k(pieces, h):
    lane = lax.broadcasted_iota(jnp.int32, (1, LANES), 1)
    hi, mid, lo = (-p[:, h:h + 1] for p in pieces)
    ones = jnp.where((lane >= HEAD_DIM) & (lane < HEAD_DIM + 3), 1.0, 0.0)
    return jnp.where(lane == HEAD_DIM + 3, hi, jnp.where(lane == HEAD_DIM + 4, mid,
                     jnp.where(lane == HEAD_DIM + 5, lo, ones)))


def _sigmoid(x):
    return 0.5 * jnp.tanh(0.5 * x) + 0.5


def _log_sigmoid(x):
    return jnp.minimum(x, 0.0) - jnp.log1p(jnp.exp(-jnp.abs(x)))


def _ada_kernel(c_ref, w_ref, b_ref, o_ref):
    h = jax.nn.silu(c_ref[...]).astype(BF16)
    o_ref[0] = jnp.dot(h, w_ref[...].astype(BF16), preferred_element_type=F32) + b_ref[...]


def _ada(c, w_ada, b_ada):
    rows = c.shape[0]
    n = w_ada.shape[1]
    tn = D_MODEL
    return pl.pallas_call(
        _ada_kernel,
        grid=(n // tn,),
        in_specs=[pl.BlockSpec((rows, D_MODEL), lambda j: (0, 0)),
                  pl.BlockSpec((D_MODEL, tn), lambda j: (0, j)),
                  pl.BlockSpec((1, tn), lambda j: (0, j))],
        out_specs=pl.BlockSpec((1, rows, tn), lambda j: (j, 0, 0)),
        out_shape=jax.ShapeDtypeStruct((n // tn, rows, tn), F32),
        compiler_params=_params(("arbitrary",)),
        name="ada_modulation",
    )(c, w_ada, b_ada.reshape(1, n))


def _flat(x3):
    return x3.reshape(x3.shape[0] * x3.shape[1], x3.shape[2])


def _merge_rows(a_ref, gas_ref, mb_ref, wpa_ref, wo_ref):
    pa = jnp.dot(_flat(a_ref[...]), wpa_ref[...], preferred_element_type=F32)
    m = _flat(gas_ref[...]).astype(F32) * pa + _flat(mb_ref[...]).astype(F32)
    return jnp.dot(m.astype(BF16), wo_ref[...], preferred_element_type=F32)


def _ffn_kernel(*refs, merge):
    if merge:
        (x_ref, a_ref, gas_ref, mb_ref, mgate_ref, wpa_ref, wo_ref,
         shift_ref, scale_ref, gate_ref, g_ref, wg_ref, wu_ref, wd_ref, o_ref) = refs
        x3 = x_ref[...]
        x3 = x3 + mgate_ref[0] * _merge_rows(a_ref, gas_ref, mb_ref, wpa_ref, wo_ref).reshape(x3.shape)
    else:
        x_ref, shift_ref, scale_ref, gate_ref, g_ref, wg_ref, wu_ref, wd_ref, o_ref = refs
        x3 = x_ref[...]
    h = _flat((_rms(x3) * g_ref[...]) * (1.0 + scale_ref[0]) + shift_ref[0]).astype(BF16)
    y = None
    for c in range(0, D_FF, FF_CHUNK):
        a = jax.nn.silu(jnp.dot(h, wg_ref[:, c:c + FF_CHUNK], preferred_element_type=F32))
        a = a * jnp.dot(h, wu_ref[:, c:c + FF_CHUNK], preferred_element_type=F32)
        d = jnp.dot(a.astype(BF16), wd_ref[c:c + FF_CHUNK, :], preferred_element_type=F32)
        y = d if y is None else y + d
    o_ref[...] = x3 + 0.5 * gate_ref[0] * y.reshape(x3.shape)


def _mod_spec(k, nbk, row0):
    return pl.BlockSpec((1, nbk, 1, D_MODEL), lambda b, i: (k, row0 // nbk + b, 0, 0))


def _ffn(x, mods, first, row0, g, wg, wu, wd, tm, nbk, merge=None):
    nb, s, _ = x.shape
    row = lambda w: pl.BlockSpec((nbk, tm, w), lambda b, i: (b, i, 0))
    weights = [_const_spec((1, 1, D_MODEL)), _const_spec(wg.shape), _const_spec(wu.shape), _const_spec(wd.shape)]
    args = [mods, mods, mods, g.reshape(1, 1, D_MODEL), wg, wu, wd]
    in_specs = [_mod_spec(first + k, nbk, row0) for k in range(3)] + weights
    if merge is not None:
        a, gas, mb, mgate, wpa, wo = merge
        args = [a, gas, mb, mods, wpa, wo] + args
        in_specs = [row(WIDTH_A), row(D_MODEL), row(D_MODEL), _mod_spec(mgate, nbk, row0), _const_spec(wpa.shape),
                    _const_spec(wo.shape)] + in_specs
    return pl.pallas_call(
        functools.partial(_ffn_kernel, merge=merge is not None),
        grid=(nb // nbk, s // tm),
        in_specs=[row(D_MODEL)] + in_specs,
        out_specs=row(D_MODEL),
        out_shape=jax.ShapeDtypeStruct(x.shape, F32),
        compiler_params=_params(("arbitrary", "arbitrary")),
        name="ffn_half_step_merged" if merge is not None else "ffn_half_step",
    )(x, *args)


def _mix_project(x_ref, shift_ref, scale_ref, g_ref, wcat_ref, z_ref):
    n = _flat((_rms(x_ref[...]) * g_ref[...]) * (1.0 + scale_ref[0]) + shift_ref[0])
    z_ref[...] = jnp.dot(n.astype(BF16), wcat_ref[...], preferred_element_type=F32)


def _mix_finish(z_ref, first_tile, bf_ref, gq_ref, gk_ref, gv_ref, wsp_ref, bsp_ref, wpb_ref, outs,
                *, tm, mix_len, mix_period, prompt):
    if prompt:
        qa_ref, ka_ref, va_ref, ko_ref, vo_ref, lf_ref, gas_ref, mb_ref, carry_ref = outs
    else:
        qo_ref, ko_ref, vo_ref, lf_ref, vbn_ref, gas_ref, mb_ref = outs
    o_q, o_k, o_v = 0, WIDTH_A, 2 * WIDTH_A
    o_zb = 3 * WIDTH_A
    o_ga = o_zb + 2 * WIDTH_B
    o_gb = o_ga + D_MODEL
    o_f = o_gb + D_MODEL

    lane = lax.broadcasted_iota(jnp.int32, (1, LANES), 1)
    logf = jnp.where(lane < N_HEADS, _log_sigmoid(z_ref[:, o_f:o_f + LANES] + bf_ref[...]), 0.0)
    if prompt:
        lf_ref[0] = logf.T[:N_HEADS, :]
    else:
        lf_ref[0] = logf

    if prompt:
        c = _cumsum_rows(logf) + jnp.where(first_tile, 0.0, carry_ref[...])
        carry_ref[...] = c[tm - 1:tm, :]
        pieces = _split3(c * LOG2E)

    for p in range(N_PAIRS):
        sl = slice(p * LANES, (p + 1) * LANES)
        qn = _head_norm(z_ref[:, o_q + p * LANES:o_q + (p + 1) * LANES], gq_ref[...])
        kn = _head_norm(z_ref[:, o_k + p * LANES:o_k + (p + 1) * LANES], gk_ref[...])
        v = z_ref[:, o_v + p * LANES:o_v + (p + 1) * LANES]
        ko_ref[0, :, sl] = kn
        vo_ref[0, :, sl] = v
        if prompt:
            he, ho = 2 * p, 2 * p + 1
            qe, qo = _head_tiles(qn * (ATTN_SCALE * LOG2E), _aug_q(pieces, he), _aug_q(pieces, ho))
            ke, ko = _head_tiles(kn, _aug_k(pieces, he), _aug_k(pieces, ho))
            ve, vo = _head_tiles(v, 1.0, 1.0)
            qa_ref[0, he] = qe.astype(BF16)
            qa_ref[0, ho] = qo.astype(BF16)
            ka_ref[0, he] = ke.astype(BF16)
            ka_ref[0, ho] = ko.astype(BF16)
            va_ref[0, he] = ve.astype(BF16)
            va_ref[0, ho] = vo.astype(BF16)
        else:
            qo_ref[0, :, sl] = qn

    zb = jax.nn.gelu(z_ref[:, o_zb:o_zb + 2 * WIDTH_B])
    u = zb[:, :WIDTH_B]
    vbn = _rms(zb[:, WIDTH_B:]) * gv_ref[...]
    if not prompt:
        vbn_ref[0] = vbn
    vbb = vbn.astype(BF16)
    keep = _tril_ones(mix_len, mix_period)
    wm = [jnp.where(keep, wsp_ref[g], 0.0).astype(BF16) for g in range(N_GROUPS)]
    rows = []
    for c0 in range(0, tm, mix_len):
        cols = [jnp.dot(wm[g], vbb[c0:c0 + mix_len, g * GROUP_DIM:(g + 1) * GROUP_DIM],
                        preferred_element_type=F32) for g in range(N_GROUPS)]
        rows.append(jnp.concatenate(cols, axis=1) + bsp_ref[...])
    mixed = rows[0] if len(rows) == 1 else jnp.concatenate(rows, axis=0)
    b = (u * mixed).astype(BF16)
    pb = jnp.dot(b, wpb_ref[...], preferred_element_type=F32)
    mb_ref[0] = (_sigmoid(z_ref[:, o_gb:o_gb + D_MODEL]) * pb).astype(BF16)
    gas_ref[0] = _sigmoid(z_ref[:, o_ga:o_ga + D_MODEL]).astype(BF16)


def _mix_kernel(x_ref, shift_ref, scale_ref, g_ref, wcat_ref, bf_ref, gq_ref, gk_ref, gv_ref,
                wsp_ref, bsp_ref, wpb_ref, *rest, n_tiles, **static):
    n_z = min(n_tiles, 2)
    z_refs = rest[-n_z:]
    outs = rest[:-n_z]
    project = functools.partial(_mix_project, x_ref, shift_ref, scale_ref, g_ref, wcat_ref)
    finish = functools.partial(_mix_finish, bf_ref=bf_ref, gq_ref=gq_ref, gk_ref=gk_ref, gv_ref=gv_ref,
                               wsp_ref=wsp_ref, bsp_ref=bsp_ref, wpb_ref=wpb_ref, outs=outs, **static)
    i = pl.program_id(1)
    if n_tiles == 1:
        project(z_refs[0])
        finish(z_refs[0], True)
        return

    @pl.when(i == 0)
    def _():
        project(z_refs[0])

    for parity in range(2):
        @pl.when((i > 0) & (i < n_tiles) & (i % 2 == parity))
        def _():
            project(z_refs[parity])
            finish(z_refs[1 - parity], i == 1)

    @pl.when(i == n_tiles)
    def _():
        finish(z_refs[(n_tiles - 1) % 2], n_tiles == 1)


def _mix(x, mods, first, row0, g, wcat, bf, gq, gk, gv, wsp, bsp, wpb, *, rows, nbk, mix_len, mix_period, prompt):
    nb, s, _ = x.shape
    tm = nbk * rows
    n_tiles = s // rows
    steps = n_tiles if n_tiles == 1 else n_tiles + 1
    done = lambda i: jnp.maximum(i - 1, 0) if n_tiles > 1 else i
    row = lambda w: pl.BlockSpec((1, tm, w), lambda b, i: (b, done(i), 0))
    head = pl.BlockSpec((1, N_HEADS, tm, LANES), lambda b, i: (b, 0, done(i), 0))
    x_spec = pl.BlockSpec((nbk, rows, D_MODEL), lambda b, i: (b, jnp.minimum(i, n_tiles - 1), 0))
    sds = jax.ShapeDtypeStruct
    tok = lambda w, dt: sds((nb // nbk, nbk * s, w), dt)
    hm = sds((nb // nbk, N_HEADS, nbk * s, LANES), BF16)
    if prompt:
        lf_spec = pl.BlockSpec((1, N_HEADS, tm), lambda b, i: (b, 0, done(i)))
        out_specs = [head, head, head, row(WIDTH_A), row(WIDTH_A), lf_spec, row(D_MODEL), row(D_MODEL)]
        out_shape = [hm, hm, hm, tok(WIDTH_A, F32), tok(WIDTH_A, F32), sds((nb, N_HEADS, s), F32),
                     tok(D_MODEL, BF16), tok(D_MODEL, BF16)]
        scratch = [pltpu.VMEM((1, LANES), F32)]
    else:
        out_specs = [row(WIDTH_A), row(WIDTH_A), row(WIDTH_A), row(LANES), row(WIDTH_B),
                     row(D_MODEL), row(D_MODEL)]
        out_shape = [tok(WIDTH_A, F32), tok(WIDTH_A, F32), tok(WIDTH_A, F32), tok(LANES, F32),
                     tok(WIDTH_B, F32), tok(D_MODEL, BF16), tok(D_MODEL, BF16)]
        scratch = []
    scratch += [pltpu.VMEM((tm, wcat.shape[1]), F32)] * min(n_tiles, 2)
    kern = functools.partial(_mix_kernel, n_tiles=n_tiles, tm=tm, mix_len=mix_len, mix_period=mix_period,
                             prompt=prompt)
    return pl.pallas_call(
        kern,
        grid=(nb // nbk, steps),
        in_specs=[x_spec, _mod_spec(first, nbk, row0), _mod_spec(first + 1, nbk, row0), _const_spec((1, 1, D_MODEL)),
                  _const_spec(wcat.shape), _const_spec(bf.shape), _const_spec(gq.shape),
                  _const_spec(gk.shape), _const_spec(gv.shape), _const_spec(wsp.shape),
                  _const_spec(bsp.shape), _const_spec(wpb.shape)],
        out_specs=out_specs,
        out_shape=out_shape,
        scratch_shapes=scratch,
        compiler_params=_params(("arbitrary", "arbitrary")),
        name="mixer_in_prompt" if prompt else "mixer_in_sample",
    )(x, mods, mods, g.reshape(1, 1, D_MODEL), wcat, bf, gq, gk, gv, wsp, bsp, wpb)


def _scores(q, k):
    return lax.dot_general(q, k, (((1,), (1,)), ((), ())), preferred_element_type=F32)


def _softmax_update(s, v, m_old, acc_old, mask):
    if mask is not None:
        s = jnp.where(mask, s, -jnp.inf)
    m_new = jnp.maximum(m_old, jnp.max(s, axis=-1, keepdims=True))
    cols = s.shape[1]
    m_wide = jnp.concatenate([m_new] * (cols // LANES), axis=1) if cols >= LANES else m_new[:, :cols]
    p = jnp.exp2(s - m_wide)
    alpha = jnp.exp2(m_old - m_new)
    acc = alpha * acc_old + jnp.dot(p.astype(BF16), v, preferred_element_type=F32)
    return m_new, acc


def _pair_output(acc_even, acc_odd):
    lane = lax.broadcasted_iota(jnp.int32, (1, LANES), 1)
    oe = acc_even * pltpu.roll(1.0 / acc_even, HEAD_DIM, 1)
    oo = acc_odd * pltpu.roll(1.0 / acc_odd, HEAD_DIM, 1)
    return jnp.where(lane < HEAD_DIM, oe, pltpu.roll(oo, HEAD_DIM, 1))


def _attn_kernel(q_ref, qn_ref, kin_ref, vin_ref, o_ref, k_ref, v_ref, s0_ref, s1_ref, acc_ref, m_ref, *, tq, tk):
    qi = pl.program_id(2)
    m_ref[...] = jnp.full(m_ref.shape, -jnp.inf, F32)
    acc_ref[...] = jnp.zeros(acc_ref.shape, F32)
    own = pl.ds(pl.multiple_of(qi * tq, tq), tq)
    k_ref[:, own, :] = kin_ref[0]
    v_ref[:, own, :] = vin_ref[0]

    def keys(j):
        return pl.ds(pl.multiple_of(j * tk, tk), tk)

    def qk(src_ref, j, s_ref, r0=0):
        for h in range(2):
            s_ref[h, r0:, :] = _scores(src_ref[0, h, r0:, :], k_ref[h, keys(j), :])

    def soft_pv(j, s_ref, mask, r0=0):
        for h in range(2):
            m_new, acc = _softmax_update(s_ref[h, r0:, :], v_ref[h, keys(j), :],
                                         m_ref[h, r0:, :], acc_ref[h, r0:, :], mask)
            m_ref[h, r0:, :] = m_new
            acc_ref[h, r0:, :] = acc

    @pl.when(qi == 0)
    def _():
        qk(q_ref, 0, s0_ref)

    def body(u, carry):
        j = 2 * u
        qk(q_ref, j + 1, s1_ref)
        soft_pv(j, s0_ref, None)
        qk(q_ref, j + 2, s0_ref)
        soft_pv(j + 1, s1_ref, None)
        return carry

    def body_quad(w, carry):
        for r in range(4):
            body(4 * w + r, carry)
        return carry

    lax.fori_loop(0, qi // 4, body_quad, 0)

    def diagonal():
        row = lax.broadcasted_iota(jnp.int32, (tq, tk), 0)
        col = lax.broadcasted_iota(jnp.int32, (tq, tk), 1)
        qk(q_ref, 2 * qi + 1, s1_ref, tk)
        soft_pv(2 * qi, s0_ref, col <= row)
        qk(qn_ref, 0, s0_ref)
        soft_pv(2 * qi + 1, s1_ref, _tril_ones(tk), tk)

    for left in range(4):
        @pl.when(qi % 4 == left)
        def _():
            for r in range(left):
                body(qi - left + r, 0)
            diagonal()

    o_ref[0] = _pair_output(acc_ref[0], acc_ref[1]).astype(o_ref.dtype)


def _attn(qa, ka, va, tk):
    nb, _, s, _ = qa.shape
    tq = 2 * tk
    nq = s // tq
    kv = pl.BlockSpec((1, 2, tq, LANES), lambda b, p, i: (b, p, i, 0))
    return pl.pallas_call(
        functools.partial(_attn_kernel, tq=tq, tk=tk),
        grid=(nb, N_PAIRS, nq),
        in_specs=[pl.BlockSpec((1, 2, tq, LANES), lambda b, p, i: (b, p, i, 0)),
                  pl.BlockSpec((1, 2, tq, LANES), lambda b, p, i: (b, p, jnp.minimum(i + 1, nq - 1), 0)),
                  kv, kv],
        out_specs=pl.BlockSpec((1, tq, LANES), lambda b, p, i: (b, i, p)),
        out_shape=jax.ShapeDtypeStruct((nb, s, WIDTH_A), BF16),
        scratch_shapes=[pltpu.VMEM((2, s, LANES), BF16), pltpu.VMEM((2, s, LANES), BF16),
                        pltpu.VMEM((2, tq, tk), F32), pltpu.VMEM((2, tq, tk), F32),
                        pltpu.VMEM((2, tq, LANES), F32), pltpu.VMEM((2, tq, LANES), F32)],
        compiler_params=_params(("arbitrary", "arbitrary", "arbitrary")),
        name="fox_prompt_attention",
    )(qa, qa, ka, va)


def _cumsum_lanes(x):
    n = x.shape[1]
    lane = lax.broadcasted_iota(jnp.int32, x.shape, 1)
    k = 1
    while k < n:
        x = x + jnp.where(lane >= k, pltpu.roll(x, k, 1), 0.0)
        k *= 2
    return x


def _attn_sample_kernel(q_ref, k_ref, v_ref, lf_ref, kc_ref, vc_ref, lfc_ref, o_ref, *, t_new):
    lfc = lfc_ref[0]
    c_cache = _cumsum_lanes(lfc)
    p_len = lfc.shape[1]
    tail = (c_cache[:, p_len - 1:p_len] - c_cache) * LOG2E
    c_rel = _cumsum_rows(lf_ref[0][:, :N_HEADS])
    pieces = _split3(c_rel * LOG2E)
    causal = _tril_ones(t_new)
    lane = lax.broadcasted_iota(jnp.int32, (1, LANES), 1)
    lo = lane < HEAD_DIM
    widen = lambda x: jnp.pad(x, ((0, 0), (0, LANES - HEAD_DIM)))
    for p in range(N_PAIRS):
        sl = slice(p * LANES, (p + 1) * LANES)
        he, ho = 2 * p, 2 * p + 1
        q_t = _head_tiles(q_ref[0, :, sl] * (ATTN_SCALE * LOG2E), _aug_q(pieces, he), _aug_q(pieces, ho))
        kn_t = _head_tiles(k_ref[0, :, sl], _aug_k(pieces, he), _aug_k(pieces, ho))
        vn_t = _head_tiles(v_ref[0, :, sl], 1.0, 1.0)
        res = []
        for j, h in enumerate((he, ho)):
            q = q_t[j].astype(BF16)
            s_c = jnp.dot(q[:, :HEAD_DIM], kc_ref[0, h].astype(BF16), preferred_element_type=F32)
            s_c = s_c + (c_rel[:, h:h + 1] * LOG2E + tail[h:h + 1, :])
            s_n = jnp.where(causal, _scores(q, kn_t[j].astype(BF16)), -jnp.inf)
            m = jnp.maximum(jnp.max(s_c, axis=-1, keepdims=True), jnp.max(s_n, axis=-1, keepdims=True))
            p_c = jnp.exp2(s_c - m)
            p_n = jnp.exp2(s_n - m)
            o_c = lax.dot_general(p_c.astype(BF16), vc_ref[0, h].astype(BF16), (((1,), (1,)), ((), ())),
                                  preferred_element_type=F32)
            acc = jnp.dot(p_n.astype(BF16), vn_t[j].astype(BF16), preferred_element_type=F32)
            acc = acc + jnp.where(lo, widen(o_c), jnp.sum(p_c, axis=-1, keepdims=True))
            res.append(acc)
        o_ref[0, :, sl] = _pair_output(res[0], res[1]).astype(o_ref.dtype)


def _attn_sample(q, k, v, lf, kc_t, vc_t, lfc_t):
    nb, t_new, _ = q.shape
    p_len = kc_t.shape[-1]
    new = lambda w: pl.BlockSpec((1, t_new, w), lambda b: (b, 0, 0))
    cache = pl.BlockSpec((1, N_HEADS, HEAD_DIM, p_len), lambda b: (b, 0, 0, 0))
    return pl.pallas_call(
        functools.partial(_attn_sample_kernel, t_new=t_new),
        grid=(nb,),
        in_specs=[new(WIDTH_A), new(WIDTH_A), new(WIDTH_A), new(LANES), cache, cache,
                  pl.BlockSpec((1, N_HEADS, p_len), lambda b: (b, 0, 0))],
        out_specs=new(WIDTH_A),
        out_shape=jax.ShapeDtypeStruct((nb, t_new, WIDTH_A), BF16),
        compiler_params=_params(("arbitrary",)),
        name="fox_sample_attention",
    )(q, k, v, lf, kc_t, vc_t, lfc_t)


def _pad_lanes(v, fill=0.0):
    v = v.reshape(1, -1)
    return jnp.pad(v, ((0, 0), (0, LANES - v.shape[1])), constant_values=fill)


def _layer(xp, xs, mods, row0_s, cache_k, cache_v, cache_lf, g1, wg1, wu1, wd1, gm, w_in, b_forget, g_q, g_k, g_v,
           w_spatial, b_spatial, w_proj_a, w_proj_b, w_out, g2, wg2, wu2, wd2, *, tm, t_attn):
    nb_p, s_p, _ = xp.shape
    nb_s, t_new, _ = xs.shape
    rows_s = nb_s * t_new

    wg1, wu1, wd1 = wg1.astype(BF16), wu1.astype(BF16), wd1.astype(BF16)
    wg2, wu2, wd2 = wg2.astype(BF16), wu2.astype(BF16), wd2.astype(BF16)
    o_f = 3 * WIDTH_A
    o_zb = o_f + N_HEADS
    wf = jnp.pad(w_in[:, o_f:o_zb], ((0, 0), (0, LANES - N_HEADS)))
    wcat = jnp.concatenate([w_in[:, :o_f], w_in[:, o_zb:], wf], axis=1).astype(BF16)
    bf = _pad_lanes(b_forget)
    gq = jnp.tile(g_q, 2).reshape(1, LANES)
    gk = jnp.tile(g_k, 2).reshape(1, LANES)
    gv = g_v.reshape(1, WIDTH_B)
    bsp = jnp.repeat(b_spatial.T, GROUP_DIM, axis=1)
    wpa = w_proj_a.astype(BF16)
    wpb = w_proj_b.astype(BF16)
    wo = w_out.astype(BF16)

    xs = _ffn(xs, mods, 0, row0_s, g1, wg1, wu1, wd1, t_new, nb_s)
    wsp_s = jnp.tile(w_spatial[:, :t_new, :t_new], (1, nb_s, nb_s))
    bsp_s = jnp.tile(bsp[:t_new], (nb_s, 1))
    q_s, k_s, v_s, lf_s, vbn_s, gas_s, mb_s = _mix(
        xs, mods, 3, row0_s, gm, wcat, bf, gq, gk, gv, wsp_s, bsp_s, wpb,
        rows=t_new, nbk=nb_s, mix_len=rows_s, mix_period=t_new, prompt=False)
    per_req = lambda z: z.reshape(nb_s, t_new, z.shape[-1])
    a_s = _attn_sample(per_req(q_s), per_req(k_s), per_req(v_s), per_req(lf_s),
                       jnp.transpose(cache_k, (0, 2, 3, 1)), jnp.transpose(cache_v, (0, 2, 3, 1)),
                       jnp.transpose(cache_lf, (0, 2, 1)))
    xs = _ffn(xs, mods, 6, row0_s, g2, wg2, wu2, wd2, t_new, nb_s,
              merge=(a_s, per_req(gas_s), per_req(mb_s), 5, wpa, wo))

    xp = _ffn(xp, mods, 0, 0, g1, wg1, wu1, wd1, min(2 * tm, s_p), 1)
    qa, ka, va, k_p, v_p, lf_p, gas, mb = _mix(
        xp, mods, 3, 0, gm, wcat, bf, gq, gk, gv, w_spatial, bsp, wpb,
        rows=tm, nbk=1, mix_len=GMLP_CHUNK, mix_period=GMLP_CHUNK, prompt=True)
    a = _attn(qa, ka, va, t_attn)
    xp = _ffn(xp, mods, 6, 0, g2, wg2, wu2, wd2, tm, 1, merge=(a, gas, mb, 5, wpa, wo))

    heads = lambda z, nb, s: z.reshape(nb, s, N_HEADS, HEAD_DIM)
    outs = (heads(k_p, nb_p, s_p), heads(v_p, nb_p, s_p), jnp.transpose(lf_p, (0, 2, 1)),
            heads(per_req(k_s), nb_s, t_new), heads(per_req(v_s), nb_s, t_new),
            per_req(lf_s)[:, :, :N_HEADS], per_req(vbn_s))
    return xp, xs, outs


def kernel(x_prompt, x_sample, c_prompt, c_sample, cache_fox_k, cache_fox_v, cache_fox_logf, w_ada, b_ada, g_norm_ffn1, w_ffn1_gate, w_ffn1_up, w_ffn1_down, g_norm_mix, w_in, b_forget, g_q, g_k, g_gmlp_v, w_spatial, b_spatial, w_proj_a, w_proj_b, w_out, g_norm_ffn2, w_ffn2_gate, w_ffn2_up, w_ffn2_down):
    depth = w_ada.shape[0]
    nb_p, s_p, _ = x_prompt.shape
    nb_s = x_sample.shape[0]
    tm = min(512, s_p)
    t_attn = min(512, s_p // 2)
    xp, xs = x_prompt, x_sample
    row0_s = -(-nb_p // nb_s) * nb_s
    c_all = jnp.concatenate([jnp.pad(c_prompt, ((0, row0_s - nb_p), (0, 0))), c_sample], axis=0)
    c_all = jnp.pad(c_all, ((0, -c_all.shape[0] % SUBLANES), (0, 0)))
    stacks = [[] for _ in range(7)]
    for l in range(depth):
        mods = _ada(c_all, w_ada[l], b_ada[l])[:, :, None, :]
        xp, xs, outs = _layer(
            xp, xs, mods, row0_s, cache_fox_k[l], cache_fox_v[l], cache_fox_logf[l],
            g_norm_ffn1[l], w_ffn1_gate[l], w_ffn1_up[l], w_ffn1_down[l],
            g_norm_mix[l], w_in[l], b_forget[l], g_q[l], g_k[l], g_gmlp_v[l], w_spatial[l], b_spatial[l],
            w_proj_a[l], w_proj_b[l], w_out[l],
            g_norm_ffn2[l], w_ffn2_gate[l], w_ffn2_up[l], w_ffn2_down[l], tm=tm, t_attn=t_attn)
        for st, o in zip(stacks, outs):
            st.append(o)
    return (xp, xs) + tuple(jnp.stack(st) for st in stacks)
```

```python
import functools

import jax
import jax.numpy as jnp
from jax import lax
from jax.experimental import pallas as pl
from jax.experimental.pallas import tpu as pltpu

D_MODEL = 1024
N_HEADS = 8
HEAD_DIM = 64
WIDTH_A = N_HEADS * HEAD_DIM
N_GROUPS = 4
GROUP_DIM = 128
WIDTH_B = N_GROUPS * GROUP_DIM
GMLP_CHUNK = 128
D_FF = 2816
FF_CHUNK = 256
N_MOD = 9
EPS = 1e-6
ATTN_SCALE = HEAD_DIM ** -0.5
LOG2E = 1.4426950408889634

LANES = 128
SUBLANES = 8
N_PAIRS = N_HEADS // 2
VMEM_LIMIT = 56 * 1024 * 1024

F32 = jnp.float32
BF16 = jnp.bfloat16


def _const_spec(shape):
    nd = len(shape)
    return pl.BlockSpec(shape, lambda *_: (0,) * nd, pipeline_mode=pl.Buffered(1))


def _params(sem, vmem_limit=VMEM_LIMIT):
    return pltpu.CompilerParams(dimension_semantics=sem, vmem_limit_bytes=vmem_limit)


def _rms(x):
    return x * lax.rsqrt(jnp.mean(x * x, axis=-1, keepdims=True) + EPS)


def _split3(c):
    hi = c.astype(BF16).astype(F32)
    r = c - hi
    mid = r.astype(BF16).astype(F32)
    lo = (r - mid).astype(BF16).astype(F32)
    return hi, mid, lo


def _tril_ones(n, period=None):
    row = lax.broadcasted_iota(jnp.int32, (n, n), 0)
    col = lax.broadcasted_iota(jnp.int32, (n, n), 1)
    keep = col <= row
    if period is not None and period != n:
        keep = keep & ((row // period) == (col // period))
    return keep


def _cumsum_rows(x):
    n, width = x.shape
    groups = n // SUBLANES
    x3 = x.reshape(groups, SUBLANES, width)
    sub = lax.broadcasted_iota(jnp.int32, (1, SUBLANES, width), 1)
    k = 1
    while k < SUBLANES:
        x3 = x3 + jnp.where(sub >= k, pltpu.roll(x3, k, 1), 0.0)
        k *= 2
    tot = x3[:, SUBLANES - 1:SUBLANES, :]
    k = 1
    while k < groups:
        tot = tot + jnp.concatenate([jnp.zeros((k, 1, width), F32), tot[:groups - k]], axis=0)
        k *= 2
    before = jnp.concatenate([jnp.zeros((1, 1, width), F32), tot[:groups - 1]], axis=0)
    return (x3 + before).reshape(n, width)


def _head_norm(slab, gain):
    lane = lax.broadcasted_iota(jnp.int32, (1, LANES), 1)
    lo = lane < HEAD_DIM
    sq = slab * slab
    ss_lo = jnp.sum(jnp.where(lo, sq, 0.0), axis=-1, keepdims=True)
    ss_hi = jnp.sum(jnp.where(lo, 0.0, sq), axis=-1, keepdims=True)
    r = jnp.where(lo, lax.rsqrt(ss_lo / HEAD_DIM + EPS), lax.rsqrt(ss_hi / HEAD_DIM + EPS))
    return slab * r * gain


def _head_tiles(slab, fill_even, fill_odd):
    lane = lax.broadcasted_iota(jnp.int32, (1, LANES), 1)
    lo = lane < HEAD_DIM
    even = jnp.where(lo, slab, fill_even)
    odd = jnp.where(lo, pltpu.roll(slab, HEAD_DIM, 1), fill_odd)
    return even, odd


def _aug_q(pieces, h):
    lane = lax.broadcasted_iota(jnp.int32, (1, LANES), 1)
    hi, mid, lo = (p[:, h:h + 1] for p in pieces)
    ones = jnp.where((lane >= HEAD_DIM + 3) & (lane < HEAD_DIM + 6), 1.0, 0.0)
    return jnp.where(lane == HEAD_DIM, hi, jnp.where(lane == HEAD_DIM + 1, mid,
                     jnp.where(lane == HEAD_DIM + 2, lo, ones)))


def _aug_k(pieces, h):
    lane = lax.broadcasted_iota(jnp.int32, (1, LANES), 1)
    hi, mid, lo = (-p[:, h:h + 1] for p in pieces)
    ones = jnp.where((lane >= HEAD_DIM) & (lane < HEAD_DIM + 3), 1.0, 0.0)
    return jnp.where(lane == HEAD_DIM + 3, hi, jnp.where(lane == HEAD_DIM + 4, mid,
                     jnp.where(lane == HEAD_DIM + 5, lo, ones)))


def _sigmoid(x):
    return 0.5 * jnp.tanh(0.5 * x) + 0.5


def _log_sigmoid(x):
    return jnp.minimum(x, 0.0) - jnp.log1p(jnp.exp(-jnp.abs(x)))


def _ada_kernel(c_ref, w_ref, b_ref, o_ref):
    h = jax.nn.silu(c_ref[...]).astype(BF16)
    o_ref[0] = jnp.dot(h, w_ref[...].astype(BF16), preferred_element_type=F32) + b_ref[...]


def _ada(c, w_ada, b_ada):
    rows = c.shape[0]
    n = w_ada.shape[1]
    tn = D_MODEL
    return pl.pallas_call(
        _ada_kernel,
        grid=(n // tn,),
        in_specs=[pl.BlockSpec((rows, D_MODEL), lambda j: (0, 0)),
                  pl.BlockSpec((D_MODEL, tn), lambda j: (0, j)),
                  pl.BlockSpec((1, tn), lambda j: (0, j))],
        out_specs=pl.BlockSpec((1, rows, tn), lambda j: (j, 0, 0)),
        out_shape=jax.ShapeDtypeStruct((n // tn, rows, tn), F32),
        compiler_params=_params(("arbitrary",)),
        name="ada_modulation",
    )(c, w_ada, b_ada.reshape(1, n))


def _flat(x3):
    return x3.reshape(x3.shape[0] * x3.shape[1], x3.shape[2])


def _merge_rows(a_ref, gas_ref, mb_ref, wpa_ref, wo_ref):
    pa = jnp.dot(_flat(a_ref[...]), wpa_ref[...], preferred_element_type=F32)
    m = _flat(gas_ref[...]).astype(F32) * pa + _flat(mb_ref[...]).astype(F32)
    return jnp.dot(m.astype(BF16), wo_ref[...], preferred_element_type=F32)


def _ffn_kernel(*refs, merge):
    if merge:
        (x_ref, a_ref, gas_ref, mb_ref, mgate_ref, wpa_ref, wo_ref,
         shift_ref, scale_ref, gate_ref, g_ref, wg_ref, wu_ref, wd_ref, o_ref) = refs
        x3 = x_ref[...]
        x3 = x3 + mgate_ref[0] * _merge_rows(a_ref, gas_ref, mb_ref, wpa_ref, wo_ref).reshape(x3.shape)
    else:
        x_ref, shift_ref, scale_ref, gate_ref, g_ref, wg_ref, wu_ref, wd_ref, o_ref = refs
        x3 = x_ref[...]
    h = _flat((_rms(x3) * g_ref[...]) * (1.0 + scale_ref[0]) + shift_ref[0]).astype(BF16)
    y = None
    for c in range(0, D_FF, FF_CHUNK):
        a = jax.nn.silu(jnp.dot(h, wg_ref[:, c:c + FF_CHUNK], preferred_element_type=F32))
        a = a * jnp.dot(h, wu_ref[:, c:c + FF_CHUNK], preferred_element_type=F32)
        d = jnp.dot(a.astype(BF16), wd_ref[c:c + FF_CHUNK, :], preferred_element_type=F32)
        y = d if y is None else y + d
    o_ref[...] = x3 + 0.5 * gate_ref[0] * y.reshape(x3.shape)


def _mod_spec(k, nbk, row0):
    return pl.BlockSpec((1, nbk, 1, D_MODEL), lambda b, i: (k, row0 // nbk + b, 0, 0))


def _ffn(x, mods, first, row0, g, wg, wu, wd, tm, nbk, merge=None):
    nb, s, _ = x.shape
    row = lambda w: pl.BlockSpec((nbk, tm, w), lambda b, i: (b, i, 0))
    weights = [_const_spec((1, 1, D_MODEL)), _const_spec(wg.shape), _const_spec(wu.shape), _const_spec(wd.shape)]
    args = [mods, mods, mods, g.reshape(1, 1, D_MODEL), wg, wu, wd]
    in_specs = [_mod_spec(first + k, nbk, row0) for k in range(3)] + weights
    if merge is not None:
        a, gas, mb, mgate, wpa, wo = merge
        args = [a, gas, mb, mods, wpa, wo] + args
        in_specs = [row(WIDTH_A), row(D_MODEL), row(D_MODEL), _mod_spec(mgate, nbk, row0), _const_spec(wpa.shape),
                    _const_spec(wo.shape)] + in_specs
    return pl.pallas_call(
        functools.partial(_ffn_kernel, merge=merge is not None),
        grid=(nb // nbk, s // tm),
        in_specs=[row(D_MODEL)] + in_specs,
        out_specs=row(D_MODEL),
        out_shape=jax.ShapeDtypeStruct(x.shape, F32),
        compiler_params=_params(("arbitrary", "arbitrary")),
        name="ffn_half_step_merged" if merge is not None else "ffn_half_step",
    )(x, *args)


def _mix_project(x_ref, shift_ref, scale_ref, g_ref, wcat_ref, z_ref):
    n = _flat((_rms(x_ref[...]) * g_ref[...]) * (1.0 + scale_ref[0]) + shift_ref[0])
    z_ref[...] = jnp.dot(n.astype(BF16), wcat_ref[...], preferred_element_type=F32)


def _mix_finish(z_ref, first_tile, bf_ref, gq_ref, gk_ref, gv_ref, wsp_ref, bsp_ref, wpb_ref, outs,
                *, tm, mix_len, mix_period, prompt):
    if prompt:
        qa_ref, ka_ref, va_ref, ko_ref, vo_ref, lf_ref, gas_ref, mb_ref, carry_ref = outs
    else:
        qo_ref, ko_ref, vo_ref, lf_ref, vbn_ref, gas_ref, mb_ref = outs
    o_q, o_k, o_v = 0, WIDTH_A, 2 * WIDTH_A
    o_zb = 3 * WIDTH_A
    o_ga = o_zb + 2 * WIDTH_B
    o_gb = o_ga + D_MODEL
    o_f = o_gb + D_MODEL

    lane = lax.broadcasted_iota(jnp.int32, (1, LANES), 1)
    logf = jnp.where(lane < N_HEADS, _log_sigmoid(z_ref[:, o_f:o_f + LANES] + bf_ref[...]), 0.0)
    if prompt:
        lf_ref[0] = logf.T[:N_HEADS, :]
    else:
        lf_ref[0] = logf

    if prompt:
        c = _cumsum_rows(logf) + jnp.where(first_tile, 0.0, carry_ref[...])
        carry_ref[...] = c[tm - 1:tm, :]
        pieces = _split3(c * LOG2E)

    for p in range(N_PAIRS):
        sl = slice(p * LANES, (p + 1) * LANES)
        qn = _head_norm(z_ref[:, o_q + p * LANES:o_q + (p + 1) * LANES], gq_ref[...])
        kn = _head_norm(z_ref[:, o_k + p * LANES:o_k + (p + 1) * LANES], gk_ref[...])
        v = z_ref[:, o_v + p * LANES:o_v + (p + 1) * LANES]
        ko_ref[0, :, sl] = kn
        vo_ref[0, :, sl] = v
        if prompt:
            he, ho = 2 * p, 2 * p + 1
            qe, qo = _head_tiles(qn * (ATTN_SCALE * LOG2E), _aug_q(pieces, he), _aug_q(pieces, ho))
            ke, ko = _head_tiles(kn, _aug_k(pieces, he), _aug_k(pieces, ho))
            ve, vo = _head_tiles(v, 1.0, 1.0)
            qa_ref[0, he] = qe.astype(BF16)
            qa_ref[0, ho] = qo.astype(BF16)
            ka_ref[0, he] = ke.astype(BF16)
            ka_ref[0, ho] = ko.astype(BF16)
            va_ref[0, he] = ve.astype(BF16)
            va_ref[0, ho] = vo.astype(BF16)
        else:
            qo_ref[0, :, sl] = qn

    zb = jax.nn.gelu(z_ref[:, o_zb:o_zb + 2 * WIDTH_B])
    u = zb[:, :WIDTH_B]
    vbn = _rms(zb[:, WIDTH_B:]) * gv_ref[...]
    if not prompt:
        vbn_ref[0] = vbn
    vbb = vbn.astype(BF16)
    keep = _tril_ones(mix_len, mix_period)
    wm = [jnp.where(keep, wsp_ref[g], 0.0).astype(BF16) for g in range(N_GROUPS)]
    rows = []
    for c0 in range(0, tm, mix_len):
        cols = [jnp.dot(wm[g], vbb[c0:c0 + mix_len, g * GROUP_DIM:(g + 1) * GROUP_DIM],
                        preferred_element_type=F32) for g in range(N_GROUPS)]
        rows.append(jnp.concatenate(cols, axis=1) + bsp_ref[...])
    mixed = rows[0] if len(rows) == 1 else jnp.concatenate(rows, axis=0)
    b = (u * mixed).astype(BF16)
    pb = jnp.dot(b, wpb_ref[...], preferred_element_type=F32)
    mb_ref[0] = (_sigmoid(z_ref[:, o_gb:o_gb + D_MODEL]) * pb).astype(BF16)
    gas_ref[0] = _sigmoid(z_ref[:, o_ga:o_ga + D_MODEL]).astype(BF16)


def _mix_kernel(x_ref, shift_ref, scale_ref, g_ref, wcat_ref, bf_ref, gq_ref, gk_ref, gv_ref,
                wsp_ref, bsp_ref, wpb_ref, *rest, n_tiles, **static):
    n_z = min(n_tiles, 2)
    z_refs = rest[-n_z:]
    outs = rest[:-n_z]
    project = functools.partial(_mix_project, x_ref, shift_ref, scale_ref, g_ref, wcat_ref)
    finish = functools.partial(_mix_finish, bf_ref=bf_ref, gq_ref=gq_ref, gk_ref=gk_ref, gv_ref=gv_ref,
                               wsp_ref=wsp_ref, bsp_ref=bsp_ref, wpb_ref=wpb_ref, outs=outs, **static)
    i = pl.program_id(1)
    if n_tiles == 1:
        project(z_refs[0])
        finish(z_refs[0], True)
        return

    @pl.when(i == 0)
    def _():
        project(z_refs[0])

    for parity in range(2):
        @pl.when((i > 0) & (i < n_tiles) & (i % 2 == parity))
        def _():
            project(z_refs[parity])
            finish(z_refs[1 - parity], i == 1)

    @pl.when(i == n_tiles)
    def _():
        finish(z_refs[(n_tiles - 1) % 2], n_tiles == 1)


def _mix(x, mods, first, row0, g, wcat, bf, gq, gk, gv, wsp, bsp, wpb, *, rows, nbk, mix_len, mix_period, prompt):
    nb, s, _ = x.shape
    tm = nbk * rows
    n_tiles = s // rows
    steps = n_tiles if n_tiles == 1 else n_tiles + 1
    done = lambda i: jnp.maximum(i - 1, 0) if n_tiles > 1 else i
    row = lambda w: pl.BlockSpec((1, tm, w), lambda b, i: (b, done(i), 0))
    head = pl.BlockSpec((1, N_HEADS, tm, LANES), lambda b, i: (b, 0, done(i), 0))
    x_spec = pl.BlockSpec((nbk, rows, D_MODEL), lambda b, i: (b, jnp.minimum(i, n_tiles - 1), 0))
    sds = jax.ShapeDtypeStruct
    tok = lambda w, dt: sds((nb // nbk, nbk * s, w), dt)
    hm = sds((nb // nbk, N_HEADS, nbk * s, LANES), BF16)
    if prompt:
        lf_spec = pl.BlockSpec((1, N_HEADS, tm), lambda b, i: (b, 0, done(i)))
        out_specs = [head, head, head, row(WIDTH_A), row(WIDTH_A), lf_spec, row(D_MODEL), row(D_MODEL)]
        out_shape = [hm, hm, hm, tok(WIDTH_A, F32), tok(WIDTH_A, F32), sds((nb, N_HEADS, s), F32),
                     tok(D_MODEL, BF16), tok(D_MODEL, BF16)]
        scratch = [pltpu.VMEM((1, LANES), F32)]
    else:
        out_specs = [row(WIDTH_A), row(WIDTH_A), row(WIDTH_A), row(LANES), row(WIDTH_B),
                     row(D_MODEL), row(D_MODEL)]
        out_shape = [tok(WIDTH_A, F32), tok(WIDTH_A, F32), tok(WIDTH_A, F32), tok(LANES, F32),
                     tok(WIDTH_B, F32), tok(D_MODEL, BF16), tok(D_MODEL, BF16)]
        scratch = []
    scratch += [pltpu.VMEM((tm, wcat.shape[1]), F32)] * min(n_tiles, 2)
    kern = functools.partial(_mix_kernel, n_tiles=n_tiles, tm=tm, mix_len=mix_len, mix_period=mix_period,
                             prompt=prompt)
    return pl.pallas_call(
        kern,
        grid=(nb // nbk, steps),
        in_specs=[x_spec, _mod_spec(first, nbk, row0), _mod_spec(first + 1, nbk, row0), _const_spec((1, 1, D_MODEL)),
                  _const_spec(wcat.shape), _const_spec(bf.shape), _const_spec(gq.shape),
                  _const_spec(gk.shape), _const_spec(gv.shape), _const_spec(wsp.shape),
                  _const_spec(bsp.shape), _const_spec(wpb.shape)],
        out_specs=out_specs,
        out_shape=out_shape,
        scratch_shapes=scratch,
        compiler_params=_params(("arbitrary", "arbitrary")),
        name="mixer_in_prompt" if prompt else "mixer_in_sample",
    )(x, mods, mods, g.reshape(1, 1, D_MODEL), wcat, bf, gq, gk, gv, wsp, bsp, wpb)


def _scores(q, k):
    return lax.dot_general(q, k, (((1,), (1,)), ((), ())), preferred_element_type=F32)


def _softmax_update(s, v, m_old, acc_old, mask):
    if mask is not None:
        s = jnp.where(mask, s, -jnp.inf)
    m_new = jnp.maximum(m_old, jnp.max(s, axis=-1, keepdims=True))
    cols = s.shape[1]
    m_wide = jnp.concatenate([m_new] * (cols // LANES), axis=1) if cols >= LANES else m_new[:, :cols]
    p = jnp.exp2(s - m_wide)
    alpha = jnp.exp2(m_old - m_new)
    acc = alpha * acc_old + jnp.dot(p.astype(BF16), v, preferred_element_type=F32)
    return m_new, acc


def _pair_output(acc_even, acc_odd):
    lane = lax.broadcasted_iota(jnp.int32, (1, LANES), 1)
    oe = acc_even * pltpu.roll(1.0 / acc_even, HEAD_DIM, 1)
    oo = acc_odd * pltpu.roll(1.0 / acc_odd, HEAD_DIM, 1)
    return jnp.where(lane < HEAD_DIM, oe, pltpu.roll(oo, HEAD_DIM, 1))


def _attn_kernel(q_ref, qn_ref, kin_ref, vin_ref, o_ref, k_ref, v_ref, s0_ref, s1_ref, acc_ref, m_ref, *, tq, tk):
    qi = pl.program_id(2)
    m_ref[...] = jnp.full(m_ref.shape, -jnp.inf, F32)
    acc_ref[...] = jnp.zeros(acc_ref.shape, F32)
    own = pl.ds(pl.multiple_of(qi * tq, tq), tq)
    k_ref[:, own, :] = kin_ref[0]
    v_ref[:, own, :] = vin_ref[0]

    def keys(j):
        return pl.ds(pl.multiple_of(j * tk, tk), tk)

    def qk(src_ref, j, s_ref, r0=0):
        for h in range(2):
            s_ref[h, r0:, :] = _scores(src_ref[0, h, r0:, :], k_ref[h, keys(j), :])

    def soft_pv(j, s_ref, mask, r0=0):
        for h in range(2):
            m_new, acc = _softmax_update(s_ref[h, r0:, :], v_ref[h, keys(j), :],
                                         m_ref[h, r0:, :], acc_ref[h, r0:, :], mask)
            m_ref[h, r0:, :] = m_new
            acc_ref[h, r0:, :] = acc

    @pl.when(qi == 0)
    def _():
        qk(q_ref, 0, s0_ref)

    def body(u, carry):
        j = 2 * u
        qk(q_ref, j + 1, s1_ref)
        soft_pv(j, s0_ref, None)
        qk(q_ref, j + 2, s0_ref)
        soft_pv(j + 1, s1_ref, None)
        return carry

    def body_quad(w, carry):
        for r in range(4):
            body(4 * w + r, carry)
        return carry

    lax.fori_loop(0, qi // 4, body_quad, 0)

    @pl.when(qi % 4 >= 2)
    def _():
        first = 4 * (qi // 4)
        body(first, 0)
        body(first + 1, 0)

    def diagonal():
        row = lax.broadcasted_iota(jnp.int32, (tq, tk), 0)
        col = lax.broadcasted_iota(jnp.int32, (tq, tk), 1)
        qk(q_ref, 2 * qi + 1, s1_ref, tk)
        soft_pv(2 * qi, s0_ref, col <= row)
        qk(qn_ref, 0, s0_ref)
        soft_pv(2 * qi + 1, s1_ref, _tril_ones(tk), tk)

    @pl.when(qi % 2 == 1)
    def _():
        body(qi - 1, 0)
        diagonal()

    @pl.when(qi % 2 == 0)
    def _():
        diagonal()

    o_ref[0] = _pair_output(acc_ref[0], acc_ref[1]).astype(o_ref.dtype)


def _attn(qa, ka, va, tk):
    nb, _, s, _ = qa.shape
    tq = 2 * tk
    nq = s // tq
    kv = pl.BlockSpec((1, 2, tq, LANES), lambda b, p, i: (b, p, i, 0))
    return pl.pallas_call(
        functools.partial(_attn_kernel, tq=tq, tk=tk),
        grid=(nb, N_PAIRS, nq),
        in_specs=[pl.BlockSpec((1, 2, tq, LANES), lambda b, p, i: (b, p, i, 0)),
                  pl.BlockSpec((1, 2, tq, LANES), lambda b, p, i: (b, p, jnp.minimum(i + 1, nq - 1), 0)),
                  kv, kv],
        out_specs=pl.BlockSpec((1, tq, LANES), lambda b, p, i: (b, i, p)),
        out_shape=jax.ShapeDtypeStruct((nb, s, WIDTH_A), BF16),
        scratch_shapes=[pltpu.VMEM((2, s, LANES), BF16), pltpu.VMEM((2, s, LANES), BF16),
                        pltpu.VMEM((2, tq, tk), F32), pltpu.VMEM((2, tq, tk), F32),
                        pltpu.VMEM((2, tq, LANES), F32), pltpu.VMEM((2, tq, LANES), F32)],
        compiler_params=_params(("arbitrary", "arbitrary", "arbitrary")),
        name="fox_prompt_attention",
    )(qa, qa, ka, va)


def _cumsum_lanes(x):
    n = x.shape[1]
    lane = lax.broadcasted_iota(jnp.int32, x.shape, 1)
    k = 1
    while k < n:
        x = x + jnp.where(lane >= k, pltpu.roll(x, k, 1), 0.0)
        k *= 2
    return x


def _attn_sample_kernel(q_ref, k_ref, v_ref, lf_ref, kc_ref, vc_ref, lfc_ref, o_ref, *, t_new):
    lfc = lfc_ref[0]
    c_cache = _cumsum_lanes(lfc)
    p_len = lfc.shape[1]
    tail = (c_cache[:, p_len - 1:p_len] - c_cache) * LOG2E
    c_rel = _cumsum_rows(lf_ref[0][:, :N_HEADS])
    pieces = _split3(c_rel * LOG2E)
    causal = _tril_ones(t_new)
    lane = lax.broadcasted_iota(jnp.int32, (1, LANES), 1)
    lo = lane < HEAD_DIM
    widen = lambda x: jnp.pad(x, ((0, 0), (0, LANES - HEAD_DIM)))
    for p in range(N_PAIRS):
        sl = slice(p * LANES, (p + 1) * LANES)
        he, ho = 2 * p, 2 * p + 1
        q_t = _head_tiles(q_ref[0, :, sl] * (ATTN_SCALE * LOG2E), _aug_q(pieces, he), _aug_q(pieces, ho))
        kn_t = _head_tiles(k_ref[0, :, sl], _aug_k(pieces, he), _aug_k(pieces, ho))
        vn_t = _head_tiles(v_ref[0, :, sl], 1.0, 1.0)
        res = []
        for j, h in enumerate((he, ho)):
            q = q_t[j].astype(BF16)
            s_c = jnp.dot(q[:, :HEAD_DIM], kc_ref[0, h].astype(BF16), preferred_element_type=F32)
            s_c = s_c + (c_rel[:, h:h + 1] * LOG2E + tail[h:h + 1, :])
            s_n = jnp.where(causal, _scores(q, kn_t[j].astype(BF16)), -jnp.inf)
            m = jnp.maximum(jnp.max(s_c, axis=-1, keepdims=True), jnp.max(s_n, axis=-1, keepdims=True))
            p_c = jnp.exp2(s_c - m)
            p_n = jnp.exp2(s_n - m)
            o_c = lax.dot_general(p_c.astype(BF16), vc_ref[0, h].astype(BF16), (((1,), (1,)), ((), ())),
                                  preferred_element_type=F32)
            acc = jnp.dot(p_n.astype(BF16), vn_t[j].astype(BF16), preferred_element_type=F32)
            acc = acc + jnp.where(lo, widen(o_c), jnp.sum(p_c, axis=-1, keepdims=True))
            res.append(acc)
        o_ref[0, :, sl] = _pair_output(res[0], res[1]).astype(o_ref.dtype)


def _attn_sample(q, k, v, lf, kc_t, vc_t, lfc_t):
    nb, t_new, _ = q.shape
    p_len = kc_t.shape[-1]
    new = lambda w: pl.BlockSpec((1, t_new, w), lambda b: (b, 0, 0))
    cache = pl.BlockSpec((1, N_HEADS, HEAD_DIM, p_len), lambda b: (b, 0, 0, 0))
    return pl.pallas_call(
        functools.partial(_attn_sample_kernel, t_new=t_new),
        grid=(nb,),
        in_specs=[new(WIDTH_A), new(WIDTH_A), new(WIDTH_A), new(LANES), cache, cache,
                  pl.BlockSpec((1, N_HEADS, p_len), lambda b: (b, 0, 0))],
        out_specs=new(WIDTH_A),
        out_shape=jax.ShapeDtypeStruct((nb, t_new, WIDTH_A), BF16),
        compiler_params=_params(("arbitrary",)),
        name="fox_sample_attention",
    )(q, k, v, lf, kc_t, vc_t, lfc_t)


def _pad_lanes(v, fill=0.0):
    v = v.reshape(1, -1)
    return jnp.pad(v, ((0, 0), (0, LANES - v.shape[1])), constant_values=fill)


def _layer(xp, xs, mods, row0_s, cache_k, cache_v, cache_lf, g1, wg1, wu1, wd1, gm, w_in, b_forget, g_q, g_k, g_v,
           w_spatial, b_spatial, w_proj_a, w_proj_b, w_out, g2, wg2, wu2, wd2, *, tm, t_attn):
    nb_p, s_p, _ = xp.shape
    nb_s, t_new, _ = xs.shape
    rows_s = nb_s * t_new

    wg1, wu1, wd1 = wg1.astype(BF16), wu1.astype(BF16), wd1.astype(BF16)
    wg2, wu2, wd2 = wg2.astype(BF16), wu2.astype(BF16), wd2.astype(BF16)
    o_f = 3 * WIDTH_A
    o_zb = o_f + N_HEADS
    wf = jnp.pad(w_in[:, o_f:o_zb], ((0, 0), (0, LANES - N_HEADS)))
    wcat = jnp.concatenate([w_in[:, :o_f], w_in[:, o_zb:], wf], axis=1).astype(BF16)
    bf = _pad_lanes(b_forget)
    gq = jnp.tile(g_q, 2).reshape(1, LANES)
    gk = jnp.tile(g_k, 2).reshape(1, LANES)
    gv = g_v.reshape(1, WIDTH_B)
    bsp = jnp.repeat(b_spatial.T, GROUP_DIM, axis=1)
    wpa = w_proj_a.astype(BF16)
    wpb = w_proj_b.astype(BF16)
    wo = w_out.astype(BF16)

    xs = _ffn(xs, mods, 0, row0_s, g1, wg1, wu1, wd1, t_new, nb_s)
    wsp_s = jnp.tile(w_spatial[:, :t_new, :t_new], (1, nb_s, nb_s))
    bsp_s = jnp.tile(bsp[:t_new], (nb_s, 1))
    q_s, k_s, v_s, lf_s, vbn_s, gas_s, mb_s = _mix(
        xs, mods, 3, row0_s, gm, wcat, bf, gq, gk, gv, wsp_s, bsp_s, wpb,
        rows=t_new, nbk=nb_s, mix_len=rows_s, mix_period=t_new, prompt=False)
    per_req = lambda z: z.reshape(nb_s, t_new, z.shape[-1])
    a_s = _attn_sample(per_req(q_s), per_req(k_s), per_req(v_s), per_req(lf_s),
                       jnp.transpose(cache_k, (0, 2, 3, 1)), jnp.transpose(cache_v, (0, 2, 3, 1)),
                       jnp.transpose(cache_lf, (0, 2, 1)))
    xs = _ffn(xs, mods, 6, row0_s, g2, wg2, wu2, wd2, t_new, nb_s,
              merge=(a_s, per_req(gas_s), per_req(mb_s), 5, wpa, wo))

    xp = _ffn(xp, mods, 0, 0, g1, wg1, wu1, wd1, min(2 * tm, s_p), 1)
    qa, ka, va, k_p, v_p, lf_p, gas, mb = _mix(
        xp, mods, 3, 0, gm, wcat, bf, gq, gk, gv, w_spatial, bsp, wpb,
        rows=tm, nbk=1, mix_len=GMLP_CHUNK, mix_period=GMLP_CHUNK, prompt=True)
    a = _attn(qa, ka, va, t_attn)
    xp = _ffn(xp, mods, 6, 0, g2, wg2, wu2, wd2, tm, 1, merge=(a, gas, mb, 5, wpa, wo))

    heads = lambda z, nb, s: z.reshape(nb, s, N_HEADS, HEAD_DIM)
    outs = (heads(k_p, nb_p, s_p), heads(v_p, nb_p, s_p), jnp.transpose(lf_p, (0, 2, 1)),
            heads(per_req(k_s), nb_s, t_new), heads(per_req(v_s), nb_s, t_new),
            per_req(lf_s)[:, :, :N_HEADS], per_req(vbn_s))
    return xp, xs, outs


def kernel(x_prompt, x_sample, c_prompt, c_sample, cache_fox_k, cache_fox_v, cache_fox_logf, w_ada, b_ada, g_norm_ffn1, w_ffn1_gate, w_ffn1_up, w_ffn1_down, g_norm_mix, w_in, b_forget, g_q, g_k, g_gmlp_v, w_spatial, b_spatial, w_proj_a, w_proj_b, w_out, g_norm_ffn2, w_ffn2_gate, w_ffn2_up, w_ffn2_down):
    depth = w_ada.shape[0]
    nb_p, s_p, _ = x_prompt.shape
    nb_s = x_sample.shape[0]
    tm = min(512, s_p)
    t_attn = min(512, s_p // 2)
    xp, xs = x_prompt, x_sample
    row0_s = -(-nb_p // nb_s) * nb_s
    c_all = jnp.concatenate([jnp.pad(c_prompt, ((0, row0_s - nb_p), (0, 0))), c_sample], axis=0)
    c_all = jnp.pad(c_all, ((0, -c_all.shape[0] % SUBLANES), (0, 0)))
    stacks = [[] for _ in range(7)]
    for l in range(depth):
        mods = _ada(c_all, w_ada[l], b_ada[l])[:, :, None, :]
        xp, xs, outs = _layer(
            xp, xs, mods, row0_s, cache_fox_k[l], cache_fox_v[l], cache_fox_logf[l],
            g_norm_ffn1[l], w_ffn1_gate[l], w_ffn1_up[l], w_ffn1_down[l],
            g_norm_mix[l], w_in[l], b_forget[l], g_q[l], g_k[l], g_gmlp_v[l], w_spatial[l], b_spatial[l],
            w_proj_a[l], w_proj_b[l], w_out[l],
            g_norm_ffn2[l], w_ffn2_gate[l], w_ffn2_up[l], w_ffn2_down[l], tm=tm, t_attn=t_attn)
        for st, o in zip(stacks, outs):
            st.append(o)
    return (xp, xs) + tuple(jnp.stack(st) for st in stacks)
```

```python
import functools

import jax
import jax.numpy as jnp
from jax import lax
from jax.experimental import pallas as pl
from jax.experimental.pallas import tpu as pltpu

D_MODEL = 1024
N_HEADS = 8
HEAD_DIM = 64
WIDTH_A = N_HEADS * HEAD_DIM
N_GROUPS = 4
GROUP_DIM = 128
WIDTH_B = N_GROUPS * GROUP_DIM
GMLP_CHUNK = 128
D_FF = 2816
FF_CHUNK = 256
N_MOD = 9
EPS = 1e-6
ATTN_SCALE = HEAD_DIM ** -0.5
LOG2E = 1.4426950408889634

LANES = 128
SUBLANES = 8
N_PAIRS = N_HEADS // 2
VMEM_LIMIT = 56 * 1024 * 1024

F32 = jnp.float32
BF16 = jnp.bfloat16


def _const_spec(shape):
    nd = len(shape)
    return pl.BlockSpec(shape, lambda *_: (0,) * nd, pipeline_mode=pl.Buffered(1))


def _params(sem, vmem_limit=VMEM_LIMIT):
    return pltpu.CompilerParams(dimension_semantics=sem, vmem_limit_bytes=vmem_limit)


def _rms(x):
    return x * lax.rsqrt(jnp.mean(x * x, axis=-1, keepdims=True) + EPS)


def _split3(c):
    hi = c.astype(BF16).astype(F32)
    r = c - hi
    mid = r.astype(BF16).astype(F32)
    lo = (r - mid).astype(BF16).astype(F32)
    return hi, mid, lo


def _tril_ones(n, period=None):
    row = lax.broadcasted_iota(jnp.int32, (n, n), 0)
    col = lax.broadcasted_iota(jnp.int32, (n, n), 1)
    keep = col <= row
    if period is not None and period != n:
        keep = keep & ((row // period) == (col // period))
    return keep


def _cumsum_rows(x):
    n, width = x.shape
    groups = n // SUBLANES
    x3 = x.reshape(groups, SUBLANES, width)
    sub = lax.broadcasted_iota(jnp.int32, (1, SUBLANES, width), 1)
    k = 1
    while k < SUBLANES:
        x3 = x3 + jnp.where(sub >= k, pltpu.roll(x3, k, 1), 0.0)
        k *= 2
    tot = x3[:, SUBLANES - 1:SUBLANES, :]
    k = 1
    while k < groups:
        tot = tot + jnp.concatenate([jnp.zeros((k, 1, width), F32), tot[:groups - k]], axis=0)
        k *= 2
    before = jnp.concatenate([jnp.zeros((1, 1, width), F32), tot[:groups - 1]], axis=0)
    return (x3 + before).reshape(n, width)


def _head_norm(slab, gain):
    lane = lax.broadcasted_iota(jnp.int32, (1, LANES), 1)
    lo = lane < HEAD_DIM
    sq = slab * slab
    ss_lo = jnp.sum(jnp.where(lo, sq, 0.0), axis=-1, keepdims=True)
    ss_hi = jnp.sum(jnp.where(lo, 0.0, sq), axis=-1, keepdims=True)
    r = jnp.where(lo, lax.rsqrt(ss_lo / HEAD_DIM + EPS), lax.rsqrt(ss_hi / HEAD_DIM + EPS))
    return slab * r * gain


def _head_tiles(slab, fill_even, fill_odd):
    lane = lax.broadcasted_iota(jnp.int32, (1, LANES), 1)
    lo = lane < HEAD_DIM
    even = jnp.where(lo, slab, fill_even)
    odd = jnp.where(lo, pltpu.roll(slab, HEAD_DIM, 1), fill_odd)
    return even, odd


def _aug_q(pieces, h):
    lane = lax.broadcasted_iota(jnp.int32, (1, LANES), 1)
    hi, mid, lo = (p[:, h:h + 1] for p in pieces)
    ones = jnp.where((lane >= HEAD_DIM + 3) & (lane < HEAD_DIM + 6), 1.0, 0.0)
    return jnp.where(lane == HEAD_DIM, hi, jnp.where(lane == HEAD_DIM + 1, mid,
                     jnp.where(lane == HEAD_DIM + 2, lo, ones)))


def _aug_k(pieces, h):
    lane = lax.broadcasted_iota(jnp.int32, (1, LANES), 1)
    hi, mid, lo = (-p[:, h:h + 1] for p in pieces)
    ones = jnp.where((lane >= HEAD_DIM) & (lane < HEAD_DIM + 3), 1.0, 0.0)
    return jnp.where(lane == HEAD_DIM + 3, hi, jnp.where(lane == HEAD_DIM + 4, mid,
                     jnp.where(lane == HEAD_DIM + 5, lo, ones)))


def _sigmoid(x):
    return 0.5 * jnp.tanh(0.5 * x) + 0.5


def _log_sigmoid(x):
    return jnp.minimum(x, 0.0) - jnp.log1p(jnp.exp(-jnp.abs(x)))


def _ada_kernel(c_ref, w_ref, b_ref, o_ref):
    h = jax.nn.silu(c_ref[...]).astype(BF16)
    o_ref[0] = jnp.dot(h, w_ref[...].astype(BF16), preferred_element_type=F32) + b_ref[...]


def _ada(c, w_ada, b_ada):
    rows = c.shape[0]
    n = w_ada.shape[1]
    tn = D_MODEL
    return pl.pallas_call(
        _ada_kernel,
        grid=(n // tn,),
        in_specs=[pl.BlockSpec((rows, D_MODEL), lambda j: (0, 0)),
                  pl.BlockSpec((D_MODEL, tn), lambda j: (0, j)),
                  pl.BlockSpec((1, tn), lambda j: (0, j))],
        out_specs=pl.BlockSpec((1, rows, tn), lambda j: (j, 0, 0)),
        out_shape=jax.ShapeDtypeStruct((n // tn, rows, tn), F32),
        compiler_params=_params(("arbitrary",)),
        name="ada_modulation",
    )(c, w_ada, b_ada.reshape(1, n))


def _flat(x3):
    return x3.reshape(x3.shape[0] * x3.shape[1], x3.shape[2])


def _merge_rows(a_ref, gas_ref, mb_ref, wpa_ref, wo_ref):
    pa = jnp.dot(_flat(a_ref[...]), wpa_ref[...], preferred_element_type=F32)
    m = _flat(gas_ref[...]).astype(F32) * pa + _flat(mb_ref[...]).astype(F32)
    return jnp.dot(m.astype(BF16), wo_ref[...], preferred_element_type=F32)


def _ffn_kernel(*refs, merge):
    if merge:
        (x_ref, a_ref, gas_ref, mb_ref, mgate_ref, wpa_ref, wo_ref,
         shift_ref, scale_ref, gate_ref, g_ref, wg_ref, wu_ref, wd_ref, o_ref) = refs
        x3 = x_ref[...]
        x3 = x3 + mgate_ref[0] * _merge_rows(a_ref, gas_ref, mb_ref, wpa_ref, wo_ref).reshape(x3.shape)
    else:
        x_ref, shift_ref, scale_ref, gate_ref, g_ref, wg_ref, wu_ref, wd_ref, o_ref = refs
        x3 = x_ref[...]
    h = _flat((_rms(x3) * g_ref[...]) * (1.0 + scale_ref[0]) + shift_ref[0]).astype(BF16)
    y = None
    for c in range(0, D_FF, FF_CHUNK):
        a = jax.nn.silu(jnp.dot(h, wg_ref[:, c:c + FF_CHUNK], preferred_element_type=F32))
        a = a * jnp.dot(h, wu_ref[:, c:c + FF_CHUNK], preferred_element_type=F32)
        d = jnp.dot(a.astype(BF16), wd_ref[c:c + FF_CHUNK, :], preferred_element_type=F32)
        y = d if y is None else y + d
    o_ref[...] = x3 + 0.5 * gate_ref[0] * y.reshape(x3.shape)


def _mod_spec(k, nbk, row0):
    return pl.BlockSpec((1, nbk, 1, D_MODEL), lambda b, i: (k, row0 // nbk + b, 0, 0))


def _ffn(x, mods, first, row0, g, wg, wu, wd, tm, nbk, merge=None):
    nb, s, _ = x.shape
    row = lambda w: pl.BlockSpec((nbk, tm, w), lambda b, i: (b, i, 0))
    weights = [_const_spec((1, 1, D_MODEL)), _const_spec(wg.shape), _const_spec(wu.shape), _const_spec(wd.shape)]
    args = [mods, mods, mods, g.reshape(1, 1, D_MODEL), wg, wu, wd]
    in_specs = [_mod_spec(first + k, nbk, row0) for k in range(3)] + weights
    if merge is not None:
        a, gas, mb, mgate, wpa, wo = merge
        args = [a, gas, mb, mods, wpa, wo] + args
        in_specs = [row(WIDTH_A), row(D_MODEL), row(D_MODEL), _mod_spec(mgate, nbk, row0), _const_spec(wpa.shape),
                    _const_spec(wo.shape)] + in_specs
    return pl.pallas_call(
        functools.partial(_ffn_kernel, merge=merge is not None),
        grid=(nb // nbk, s // tm),
        in_specs=[row(D_MODEL)] + in_specs,
        out_specs=row(D_MODEL),
        out_shape=jax.ShapeDtypeStruct(x.shape, F32),
        compiler_params=_params(("arbitrary", "arbitrary")),
        name="ffn_half_step_merged" if merge is not None else "ffn_half_step",
    )(x, *args)


def _mix_project(x_ref, shift_ref, scale_ref, g_ref, wcat_ref, z_ref):
    n = _flat((_rms(x_ref[...]) * g_ref[...]) * (1.0 + scale_ref[0]) + shift_ref[0])
    z_ref[...] = jnp.dot(n.astype(BF16), wcat_ref[...], preferred_element_type=F32)


def _mix_finish(z_ref, first_tile, bf_ref, gq_ref, gk_ref, gv_ref, wsp_ref, bsp_ref, wpb_ref, outs,
                *, tm, mix_len, mix_period, prompt):
    if prompt:
        qa_ref, ka_ref, va_ref, ko_ref, vo_ref, lf_ref, gas_ref, mb_ref, carry_ref = outs
    else:
        qo_ref, ko_ref, vo_ref, lf_ref, vbn_ref, gas_ref, mb_ref = outs
    o_q, o_k, o_v = 0, WIDTH_A, 2 * WIDTH_A
    o_zb = 3 * WIDTH_A
    o_ga = o_zb + 2 * WIDTH_B
    o_gb = o_ga + D_MODEL
    o_f = o_gb + D_MODEL

    lane = lax.broadcasted_iota(jnp.int32, (1, LANES), 1)
    logf = jnp.where(lane < N_HEADS, _log_sigmoid(z_ref[:, o_f:o_f + LANES] + bf_ref[...]), 0.0)
    if prompt:
        lf_ref[0] = logf.T[:N_HEADS, :]
    else:
        lf_ref[0] = logf

    if prompt:
        c = _cumsum_rows(logf) + jnp.where(first_tile, 0.0, carry_ref[...])
        carry_ref[...] = c[tm - 1:tm, :]
        pieces = _split3(c * LOG2E)

    kn_all, v_all = [], []
    for p in range(N_PAIRS):
        sl = slice(p * LANES, (p + 1) * LANES)
        qn = _head_norm(z_ref[:, o_q + p * LANES:o_q + (p + 1) * LANES], gq_ref[...])
        kn = _head_norm(z_ref[:, o_k + p * LANES:o_k + (p + 1) * LANES], gk_ref[...])
        v = z_ref[:, o_v + p * LANES:o_v + (p + 1) * LANES]
        if prompt:
            kn_all.append(kn)
            v_all.append(v)
        else:
            ko_ref[0, :, sl] = kn
            vo_ref[0, :, sl] = v
        if prompt:
            he, ho = 2 * p, 2 * p + 1
            qe, qo = _head_tiles(qn * (ATTN_SCALE * LOG2E), _aug_q(pieces, he), _aug_q(pieces, ho))
            ke, ko = _head_tiles(kn, _aug_k(pieces, he), _aug_k(pieces, ho))
            ve, vo = _head_tiles(v, 1.0, 1.0)
            qa_ref[0, he] = qe.astype(BF16)
            qa_ref[0, ho] = qo.astype(BF16)
            ka_ref[0, he] = ke.astype(BF16)
            ka_ref[0, ho] = ko.astype(BF16)
            va_ref[0, he] = ve.astype(BF16)
            va_ref[0, ho] = vo.astype(BF16)
        else:
            qo_ref[0, :, sl] = qn

    if prompt:
        ko_ref[0] = jnp.concatenate(kn_all, axis=1).T.reshape(N_HEADS, HEAD_DIM, tm)
        vo_ref[0] = jnp.concatenate(v_all, axis=1).T.reshape(N_HEADS, HEAD_DIM, tm)

    zb = jax.nn.gelu(z_ref[:, o_zb:o_zb + 2 * WIDTH_B])
    u = zb[:, :WIDTH_B]
    vbn = _rms(zb[:, WIDTH_B:]) * gv_ref[...]
    if not prompt:
        vbn_ref[0] = vbn
    vbb = vbn.astype(BF16)
    keep = _tril_ones(mix_len, mix_period)
    wm = [jnp.where(keep, wsp_ref[g], 0.0).astype(BF16) for g in range(N_GROUPS)]
    rows = []
    for c0 in range(0, tm, mix_len):
        cols = [jnp.dot(wm[g], vbb[c0:c0 + mix_len, g * GROUP_DIM:(g + 1) * GROUP_DIM],
                        preferred_element_type=F32) for g in range(N_GROUPS)]
        rows.append(jnp.concatenate(cols, axis=1) + bsp_ref[...])
    mixed = rows[0] if len(rows) == 1 else jnp.concatenate(rows, axis=0)
    b = (u * mixed).astype(BF16)
    pb = jnp.dot(b, wpb_ref[...], preferred_element_type=F32)
    mb_ref[0] = (_sigmoid(z_ref[:, o_gb:o_gb + D_MODEL]) * pb).astype(BF16)
    gas_ref[0] = _sigmoid(z_ref[:, o_ga:o_ga + D_MODEL]).astype(BF16)


def _mix_kernel(x_ref, shift_ref, scale_ref, g_ref, wcat_ref, bf_ref, gq_ref, gk_ref, gv_ref,
                wsp_ref, bsp_ref, wpb_ref, *rest, n_tiles, **static):
    n_z = min(n_tiles, 2)
    z_refs = rest[-n_z:]
    outs = rest[:-n_z]
    project = functools.partial(_mix_project, x_ref, shift_ref, scale_ref, g_ref, wcat_ref)
    finish = functools.partial(_mix_finish, bf_ref=bf_ref, gq_ref=gq_ref, gk_ref=gk_ref, gv_ref=gv_ref,
                               wsp_ref=wsp_ref, bsp_ref=bsp_ref, wpb_ref=wpb_ref, outs=outs, **static)
    i = pl.program_id(1)
    if n_tiles == 1:
        project(z_refs[0])
        finish(z_refs[0], True)
        return

    @pl.when(i == 0)
    def _():
        project(z_refs[0])

    for parity in range(2):
        @pl.when((i > 0) & (i < n_tiles) & (i % 2 == parity))
        def _():
            project(z_refs[parity])
            finish(z_refs[1 - parity], i == 1)

    @pl.when(i == n_tiles)
    def _():
        finish(z_refs[(n_tiles - 1) % 2], n_tiles == 1)


def _mix(x, mods, first, row0, g, wcat, bf, gq, gk, gv, wsp, bsp, wpb, *, rows, nbk, mix_len, mix_period, prompt):
    nb, s, _ = x.shape
    tm = nbk * rows
    n_tiles = s // rows
    steps = n_tiles if n_tiles == 1 else n_tiles + 1
    done = lambda i: jnp.maximum(i - 1, 0) if n_tiles > 1 else i
    row = lambda w: pl.BlockSpec((1, tm, w), lambda b, i: (b, done(i), 0))
    head = pl.BlockSpec((1, N_HEADS, tm, LANES), lambda b, i: (b, 0, done(i), 0))
    x_spec = pl.BlockSpec((nbk, rows, D_MODEL), lambda b, i: (b, jnp.minimum(i, n_tiles - 1), 0))
    sds = jax.ShapeDtypeStruct
    tok = lambda w, dt: sds((nb // nbk, nbk * s, w), dt)
    hm = sds((nb // nbk, N_HEADS, nbk * s, LANES), BF16)
    if prompt:
        lf_spec = pl.BlockSpec((1, N_HEADS, tm), lambda b, i: (b, 0, done(i)))
        kv_spec = pl.BlockSpec((1, N_HEADS, HEAD_DIM, tm), lambda b, i: (b, 0, 0, done(i)))
        kv_t = sds((nb, N_HEADS, HEAD_DIM, s), F32)
        out_specs = [head, head, head, kv_spec, kv_spec, lf_spec, row(D_MODEL), row(D_MODEL)]
        out_shape = [hm, hm, hm, kv_t, kv_t, sds((nb, N_HEADS, s), F32),
                     tok(D_MODEL, BF16), tok(D_MODEL, BF16)]
        scratch = [pltpu.VMEM((1, LANES), F32)]
    else:
        out_specs = [row(WIDTH_A), row(WIDTH_A), row(WIDTH_A), row(LANES), row(WIDTH_B),
                     row(D_MODEL), row(D_MODEL)]
        out_shape = [tok(WIDTH_A, F32), tok(WIDTH_A, F32), tok(WIDTH_A, F32), tok(LANES, F32),
                     tok(WIDTH_B, F32), tok(D_MODEL, BF16), tok(D_MODEL, BF16)]
        scratch = []
    scratch += [pltpu.VMEM((tm, wcat.shape[1]), F32)] * min(n_tiles, 2)
    kern = functools.partial(_mix_kernel, n_tiles=n_tiles, tm=tm, mix_len=mix_len, mix_period=mix_period,
                             prompt=prompt)
    return pl.pallas_call(
        kern,
        grid=(nb // nbk, steps),
        in_specs=[x_spec, _mod_spec(first, nbk, row0), _mod_spec(first + 1, nbk, row0), _const_spec((1, 1, D_MODEL)),
                  _const_spec(wcat.shape), _const_spec(bf.shape), _const_spec(gq.shape),
                  _const_spec(gk.shape), _const_spec(gv.shape), _const_spec(wsp.shape),
                  _const_spec(bsp.shape), _const_spec(wpb.shape)],
        out_specs=out_specs,
        out_shape=out_shape,
        scratch_shapes=scratch,
        compiler_params=_params(("arbitrary", "arbitrary")),
        name="mixer_in_prompt" if prompt else "mixer_in_sample",
    )(x, mods, mods, g.reshape(1, 1, D_MODEL), wcat, bf, gq, gk, gv, wsp, bsp, wpb)


def _scores(q, k):
    return lax.dot_general(q, k, (((1,), (1,)), ((), ())), preferred_element_type=F32)


def _softmax_update(s, v, m_old, acc_old, mask):
    if mask is not None:
        s = jnp.where(mask, s, -jnp.inf)
    m_new = jnp.maximum(m_old, jnp.max(s, axis=-1, keepdims=True))
    cols = s.shape[1]
    m_wide = jnp.concatenate([m_new] * (cols // LANES), axis=1) if cols >= LANES else m_new[:, :cols]
    p = jnp.exp2(s - m_wide)
    alpha = jnp.exp2(m_old - m_new)
    acc = alpha * acc_old + jnp.dot(p.astype(BF16), v, preferred_element_type=F32)
    return m_new, acc


def _pair_output(acc_even, acc_odd):
    lane = lax.broadcasted_iota(jnp.int32, (1, LANES), 1)
    oe = acc_even * pltpu.roll(1.0 / acc_even, HEAD_DIM, 1)
    oo = acc_odd * pltpu.roll(1.0 / acc_odd, HEAD_DIM, 1)
    return jnp.where(lane < HEAD_DIM, oe, pltpu.roll(oo, HEAD_DIM, 1))


def _attn_kernel(q_ref, qn_ref, kin_ref, vin_ref, o_ref, k_ref, v_ref, s0_ref, s1_ref, acc_ref, m_ref, *, tq, tk):
    qi = pl.program_id(2)
    m_ref[...] = jnp.full(m_ref.shape, -jnp.inf, F32)
    acc_ref[...] = jnp.zeros(acc_ref.shape, F32)
    own = pl.ds(pl.multiple_of(qi * tq, tq), tq)
    k_ref[:, own, :] = kin_ref[0]
    v_ref[:, own, :] = vin_ref[0]

    def keys(j):
        return pl.ds(pl.multiple_of(j * tk, tk), tk)

    def qk(src_ref, j, s_ref, r0=0):
        for h in range(2):
            s_ref[h, r0:, :] = _scores(src_ref[0, h, r0:, :], k_ref[h, keys(j), :])

    def soft_pv(j, s_ref, mask, r0=0):
        for h in range(2):
            m_new, acc = _softmax_update(s_ref[h, r0:, :], v_ref[h, keys(j), :],
                                         m_ref[h, r0:, :], acc_ref[h, r0:, :], mask)
            m_ref[h, r0:, :] = m_new
            acc_ref[h, r0:, :] = acc

    @pl.when(qi == 0)
    def _():
        qk(q_ref, 0, s0_ref)

    def body(u, carry):
        j = 2 * u
        qk(q_ref, j + 1, s1_ref)
        soft_pv(j, s0_ref, None)
        qk(q_ref, j + 2, s0_ref)
        soft_pv(j + 1, s1_ref, None)
        return carry

    def body_quad(w, carry):
        for r in range(4):
            body(4 * w + r, carry)
        return carry

    lax.fori_loop(0, qi // 4, body_quad, 0)

    @pl.when(qi % 4 >= 2)
    def _():
        first = 4 * (qi // 4)
        body(first, 0)
        body(first + 1, 0)

    def diagonal():
        row = lax.broadcasted_iota(jnp.int32, (tq, tk), 0)
        col = lax.broadcasted_iota(jnp.int32, (tq, tk), 1)
        qk(q_ref, 2 * qi + 1, s1_ref, tk)
        soft_pv(2 * qi, s0_ref, col <= row)
        qk(qn_ref, 0, s0_ref)
        soft_pv(2 * qi + 1, s1_ref, _tril_ones(tk), tk)

    @pl.when(qi % 2 == 1)
    def _():
        body(qi - 1, 0)
        diagonal()

    @pl.when(qi % 2 == 0)
    def _():
        diagonal()

    o_ref[0] = _pair_output(acc_ref[0], acc_ref[1]).astype(o_ref.dtype)


def _attn(qa, ka, va, tk):
    nb, _, s, _ = qa.shape
    tq = 2 * tk
    nq = s // tq
    kv = pl.BlockSpec((1, 2, tq, LANES), lambda b, p, i: (b, p, i, 0))
    return pl.pallas_call(
        functools.partial(_attn_kernel, tq=tq, tk=tk),
        grid=(nb, N_PAIRS, nq),
        in_specs=[pl.BlockSpec((1, 2, tq, LANES), lambda b, p, i: (b, p, i, 0)),
                  pl.BlockSpec((1, 2, tq, LANES), lambda b, p, i: (b, p, jnp.minimum(i + 1, nq - 1), 0)),
                  kv, kv],
        out_specs=pl.BlockSpec((1, tq, LANES), lambda b, p, i: (b, i, p)),
        out_shape=jax.ShapeDtypeStruct((nb, s, WIDTH_A), BF16),
        scratch_shapes=[pltpu.VMEM((2, s, LANES), BF16), pltpu.VMEM((2, s, LANES), BF16),
                        pltpu.VMEM((2, tq, tk), F32), pltpu.VMEM((2, tq, tk), F32),
                        pltpu.VMEM((2, tq, LANES), F32), pltpu.VMEM((2, tq, LANES), F32)],
        compiler_params=_params(("arbitrary", "arbitrary", "arbitrary")),
        name="fox_prompt_attention",
    )(qa, qa, ka, va)


def _cumsum_lanes(x):
    n = x.shape[1]
    lane = lax.broadcasted_iota(jnp.int32, x.shape, 1)
    k = 1
    while k < n:
        x = x + jnp.where(lane >= k, pltpu.roll(x, k, 1), 0.0)
        k *= 2
    return x


def _attn_sample_kernel(q_ref, k_ref, v_ref, lf_ref, kc_ref, vc_ref, lfc_ref, o_ref, *, t_new):
    lfc = lfc_ref[0]
    c_cache = _cumsum_lanes(lfc)
    p_len = lfc.shape[1]
    tail = (c_cache[:, p_len - 1:p_len] - c_cache) * LOG2E
    c_rel = _cumsum_rows(lf_ref[0][:, :N_HEADS])
    pieces = _split3(c_rel * LOG2E)
    causal = _tril_ones(t_new)
    lane = lax.broadcasted_iota(jnp.int32, (1, LANES), 1)
    lo = lane < HEAD_DIM
    widen = lambda x: jnp.pad(x, ((0, 0), (0, LANES - HEAD_DIM)))
    for p in range(N_PAIRS):
        sl = slice(p * LANES, (p + 1) * LANES)
        he, ho = 2 * p, 2 * p + 1
        q_t = _head_tiles(q_ref[0, :, sl] * (ATTN_SCALE * LOG2E), _aug_q(pieces, he), _aug_q(pieces, ho))
        kn_t = _head_tiles(k_ref[0, :, sl], _aug_k(pieces, he), _aug_k(pieces, ho))
        vn_t = _head_tiles(v_ref[0, :, sl], 1.0, 1.0)
        res = []
        for j, h in enumerate((he, ho)):
            q = q_t[j].astype(BF16)
            s_c = jnp.dot(q[:, :HEAD_DIM], kc_ref[0, h].astype(BF16), preferred_element_type=F32)
            s_c = s_c + (c_rel[:, h:h + 1] * LOG2E + tail[h:h + 1, :])
            s_n = jnp.where(causal, _scores(q, kn_t[j].astype(BF16)), -jnp.inf)
            m = jnp.maximum(jnp.max(s_c, axis=-1, keepdims=True), jnp.max(s_n, axis=-1, keepdims=True))
            p_c = jnp.exp2(s_c - m)
            p_n = jnp.exp2(s_n - m)
            o_c = lax.dot_general(p_c.astype(BF16), vc_ref[0, h].astype(BF16), (((1,), (1,)), ((), ())),
                                  preferred_element_type=F32)
            acc = jnp.dot(p_n.astype(BF16), vn_t[j].astype(BF16), preferred_element_type=F32)
            acc = acc + jnp.where(lo, widen(o_c), jnp.sum(p_c, axis=-1, keepdims=True))
            res.append(acc)
        o_ref[0, :, sl] = _pair_output(res[0], res[1]).astype(o_ref.dtype)


def _attn_sample(q, k, v, lf, kc_t, vc_t, lfc_t):
    nb, t_new, _ = q.shape
    p_len = kc_t.shape[-1]
    new = lambda w: pl.BlockSpec((1, t_new, w), lambda b: (b, 0, 0))
    cache = pl.BlockSpec((1, N_HEADS, HEAD_DIM, p_len), lambda b: (b, 0, 0, 0))
    return pl.pallas_call(
        functools.partial(_attn_sample_kernel, t_new=t_new),
        grid=(nb,),
        in_specs=[new(WIDTH_A), new(WIDTH_A), new(WIDTH_A), new(LANES), cache, cache,
                  pl.BlockSpec((1, N_HEADS, p_len), lambda b: (b, 0, 0))],
        out_specs=new(WIDTH_A),
        out_shape=jax.ShapeDtypeStruct((nb, t_new, WIDTH_A), BF16),
        compiler_params=_params(("arbitrary",)),
        name="fox_sample_attention",
    )(q, k, v, lf, kc_t, vc_t, lfc_t)


def _pad_lanes(v, fill=0.0):
    v = v.reshape(1, -1)
    return jnp.pad(v, ((0, 0), (0, LANES - v.shape[1])), constant_values=fill)


def _layer(xp, xs, mods, row0_s, cache_k, cache_v, cache_lf, g1, wg1, wu1, wd1, gm, w_in, b_forget, g_q, g_k, g_v,
           w_spatial, b_spatial, w_proj_a, w_proj_b, w_out, g2, wg2, wu2, wd2, *, tm, t_attn):
    nb_p, s_p, _ = xp.shape
    nb_s, t_new, _ = xs.shape
    rows_s = nb_s * t_new

    wg1, wu1, wd1 = wg1.astype(BF16), wu1.astype(BF16), wd1.astype(BF16)
    wg2, wu2, wd2 = wg2.astype(BF16), wu2.astype(BF16), wd2.astype(BF16)
    o_f = 3 * WIDTH_A
    o_zb = o_f + N_HEADS
    wf = jnp.pad(w_in[:, o_f:o_zb], ((0, 0), (0, LANES - N_HEADS)))
    wcat = jnp.concatenate([w_in[:, :o_f], w_in[:, o_zb:], wf], axis=1).astype(BF16)
    bf = _pad_lanes(b_forget)
    gq = jnp.tile(g_q, 2).reshape(1, LANES)
    gk = jnp.tile(g_k, 2).reshape(1, LANES)
    gv = g_v.reshape(1, WIDTH_B)
    bsp = jnp.repeat(b_spatial.T, GROUP_DIM, axis=1)
    wpa = w_proj_a.astype(BF16)
    wpb = w_proj_b.astype(BF16)
    wo = w_out.astype(BF16)

    xs = _ffn(xs, mods, 0, row0_s, g1, wg1, wu1, wd1, t_new, nb_s)
    wsp_s = jnp.tile(w_spatial[:, :t_new, :t_new], (1, nb_s, nb_s))
    bsp_s = jnp.tile(bsp[:t_new], (nb_s, 1))
    q_s, k_s, v_s, lf_s, vbn_s, gas_s, mb_s = _mix(
        xs, mods, 3, row0_s, gm, wcat, bf, gq, gk, gv, wsp_s, bsp_s, wpb,
        rows=t_new, nbk=nb_s, mix_len=rows_s, mix_period=t_new, prompt=False)
    per_req = lambda z: z.reshape(nb_s, t_new, z.shape[-1])
    a_s = _attn_sample(per_req(q_s), per_req(k_s), per_req(v_s), per_req(lf_s),
                       jnp.transpose(cache_k, (0, 2, 3, 1)), jnp.transpose(cache_v, (0, 2, 3, 1)),
                       jnp.transpose(cache_lf, (0, 2, 1)))
    xs = _ffn(xs, mods, 6, row0_s, g2, wg2, wu2, wd2, t_new, nb_s,
              merge=(a_s, per_req(gas_s), per_req(mb_s), 5, wpa, wo))

    xp = _ffn(xp, mods, 0, 0, g1, wg1, wu1, wd1, min(2 * tm, s_p), 1)
    qa, ka, va, k_p, v_p, lf_p, gas, mb = _mix(
        xp, mods, 3, 0, gm, wcat, bf, gq, gk, gv, w_spatial, bsp, wpb,
        rows=tm, nbk=1, mix_len=GMLP_CHUNK, mix_period=GMLP_CHUNK, prompt=True)
    a = _attn(qa, ka, va, t_attn)
    xp = _ffn(xp, mods, 6, 0, g2, wg2, wu2, wd2, tm, 1, merge=(a, gas, mb, 5, wpa, wo))

    heads = lambda z, nb, s: z.reshape(nb, s, N_HEADS, HEAD_DIM)
    outs = (jnp.transpose(k_p, (0, 3, 1, 2)), jnp.transpose(v_p, (0, 3, 1, 2)), jnp.transpose(lf_p, (0, 2, 1)),
            heads(per_req(k_s), nb_s, t_new), heads(per_req(v_s), nb_s, t_new),
            per_req(lf_s)[:, :, :N_HEADS], per_req(vbn_s))
    return xp, xs, outs


def kernel(x_prompt, x_sample, c_prompt, c_sample, cache_fox_k, cache_fox_v, cache_fox_logf, w_ada, b_ada, g_norm_ffn1, w_ffn1_gate, w_ffn1_up, w_ffn1_down, g_norm_mix, w_in, b_forget, g_q, g_k, g_gmlp_v, w_spatial, b_spatial, w_proj_a, w_proj_b, w_out, g_norm_ffn2, w_ffn2_gate, w_ffn2_up, w_ffn2_down):
    depth = w_ada.shape[0]
    nb_p, s_p, _ = x_prompt.shape
    nb_s = x_sample.shape[0]
    tm = min(512, s_p)
    t_attn = min(512, s_p // 2)
    xp, xs = x_prompt, x_sample
    row0_s = -(-nb_p // nb_s) * nb_s
    c_all = jnp.concatenate([jnp.pad(c_prompt, ((0, row0_s - nb_p), (0, 0))), c_sample], axis=0)
    c_all = jnp.pad(c_all, ((0, -c_all.shape[0] % SUBLANES), (0, 0)))
    stacks = [[] for _ in range(7)]
    for l in range(depth):
        mods = _ada(c_all, w_ada[l], b_ada[l])[:, :, None, :]
        xp, xs, outs = _layer(
            xp, xs, mods, row0_s, cache_fox_k[l], cache_fox_v[l], cache_fox_logf[l],
            g_norm_ffn1[l], w_ffn1_gate[l], w_ffn1_up[l], w_ffn1_down[l],
            g_norm_mix[l], w_in[l], b_forget[l], g_q[l], g_k[l], g_gmlp_v[l], w_spatial[l], b_spatial[l],
            w_proj_a[l], w_proj_b[l], w_out[l],
            g_norm_ffn2[l], w_ffn2_gate[l], w_ffn2_up[l], w_ffn2_down[l], tm=tm, t_attn=t_attn)
        for st, o in zip(stacks, outs):
            st.append(o)
    return (xp, xs) + tuple(jnp.stack(st) for st in stacks)
```
